```python
import math
import jax, jax.numpy as jnp
from jax import lax
import numpy as np

D_MODEL = 1024
BATCH = 4
SEQ = 4096
DEPTH = 1
DEC_BATCH = 128
DEC_SEQ = 4
PAST_LEN = 8192
PAGE_SIZE = 128

N_HEADS = 8
N_KV_HEADS = 4
HEAD_DIM = D_MODEL // N_HEADS
KV_GROUP = N_HEADS // N_KV_HEADS
ROT_DIM = HEAD_DIM // 4
ROPE_THETA = 500000.0
MOBA_BLOCK = 256
MOBA_TOP_K = 3
Q_BLOCK = 128
CONV_CH = D_MODEL
CONV_WIDTH = 31
CONV_STATE = CONV_WIDTH - 1
FFN_HIDDEN = -(-8 * D_MODEL // (3 * 256)) * 256
NORM_EPS = 1e-6
NEG_INF = -1e30
POOL_FACTOR = 1.25
Q_W = N_HEADS * HEAD_DIM
KV_W = N_KV_HEADS * HEAD_DIM
IN_W = Q_W + 2 * KV_W + 2 * CONV_CH + 2 * D_MODEL
IN_SPLITS = [Q_W, Q_W + KV_W, Q_W + 2 * KV_W, Q_W + 2 * KV_W + CONV_CH,
             Q_W + 2 * KV_W + 2 * CONV_CH, Q_W + 2 * KV_W + 2 * CONV_CH + D_MODEL]

kernel_name = 'hybrid_conformer_conv_moba_decoder_step'


def rms_norm(x, g):
    xf = x.astype(jnp.float32)
    y = xf * lax.rsqrt(jnp.mean(xf * xf, axis=-1, keepdims=True) + NORM_EPS)
    return (y * g.astype(jnp.float32)).astype(x.dtype)


def layer_norm(x, g, b):
    xf = x.astype(jnp.float32)
    xc = xf - jnp.mean(xf, axis=-1, keepdims=True)
    var = jnp.mean(xc * xc, axis=-1, keepdims=True)
    return (xc * lax.rsqrt(var + NORM_EPS) * g.astype(jnp.float32) + b.astype(jnp.float32)).astype(x.dtype)


def partial_rope(x, pos):
    half = ROT_DIM // 2
    inv_freq = ROPE_THETA ** (-(jnp.arange(half, dtype=jnp.float32) * 2.0 / ROT_DIM))
    ang = pos.astype(jnp.float32)[:, None] * inv_freq[None, :]
    cos = jnp.cos(ang)[:, None, :]
    sin = jnp.sin(ang)[:, None, :]
    xr = x[..., :ROT_DIM].astype(jnp.float32)
    x1, x2 = xr[..., :half], xr[..., half:]
    rot = jnp.concatenate([x1 * cos - x2 * sin, x2 * cos + x1 * sin], axis=-1).astype(x.dtype)
    return jnp.concatenate([rot, x[..., ROT_DIM:]], axis=-1)


def moba_core(q, means, n_elig, k_blocks, v_blocks, k_own, v_own, own_mask):
    kv_of_h = np.arange(N_HEADS) // KV_GROUP
    scale = HEAD_DIM ** -0.5
    n_q = q.shape[0]
    qf = q.astype(jnp.float32)
    nbs = means.shape[0]
    scores = jnp.einsum('qhd,nhd->qhn', qf, means[:, kv_of_h])
    scores = jnp.where(jnp.arange(nbs) < n_elig, scores, NEG_INF)
    _, sel = lax.top_k(scores, MOBA_TOP_K)
    valid = jnp.arange(MOBA_TOP_K) < n_elig
    heads = kv_of_h[None, :, None]
    ks = k_blocks[heads, sel]
    vs = v_blocks[heads, sel]
    l_sel = jnp.einsum('qhd,qhjkd->qhjk', qf, ks.astype(jnp.float32)) * scale
    l_sel = jnp.where(valid[None, None, :, None], l_sel, NEG_INF)
    l_own = jnp.einsum('qhd,lhd->qhl', qf, k_own[:, kv_of_h].astype(jnp.float32)) * scale
    l_own = jnp.where(own_mask[:, None, :], l_own, NEG_INF)
    n_sel = MOBA_TOP_K * MOBA_BLOCK
    p = jax.nn.softmax(jnp.concatenate([l_sel.reshape(n_q, N_HEADS, n_sel), l_own], axis=-1), axis=-1)
    p_sel = p[..., :n_sel].reshape(n_q, N_HEADS, MOBA_TOP_K, MOBA_BLOCK).astype(v_own.dtype)
    p_own = p[..., n_sel:].astype(v_own.dtype)
    return (jnp.einsum('qhjk,qhjkd->qhd', p_sel, vs)
            + jnp.einsum('qhl,lhd->qhd', p_own, v_own[:, kv_of_h]))


def moba_prompt(q, k, v):
    B, S = q.shape[:2]
    nb = -(-S // MOBA_BLOCK)
    s_pad = nb * MOBA_BLOCK
    nbs = max(nb, MOBA_TOP_K)
    pad = ((0, 0), (0, s_pad - S), (0, 0), (0, 0))
    kp = jnp.pad(k, pad)
    vp = jnp.pad(v, pad)
    kr = kp.reshape(B, nb, MOBA_BLOCK, N_KV_HEADS, HEAD_DIM)
    vr = vp.reshape(B, nb, MOBA_BLOCK, N_KV_HEADS, HEAD_DIM)
    means = jnp.mean(kr.astype(jnp.float32), axis=2)
    means = jnp.pad(means, ((0, 0), (0, nbs - nb), (0, 0), (0, 0)))
    bpad = ((0, 0), (0, 0), (0, nbs - nb), (0, 0), (0, 0))
    kb = jnp.pad(kr.transpose(0, 3, 1, 2, 4), bpad)
    vb = jnp.pad(vr.transpose(0, 3, 1, 2, 4), bpad)
    n_chunks = S // Q_BLOCK

    def step(idx):
        b = idx // n_chunks
        q0 = (idx % n_chunks) * Q_BLOCK
        ob = q0 // MOBA_BLOCK
        qc = lax.dynamic_slice_in_dim(q[b], q0, Q_BLOCK, 0)
        k_own = lax.dynamic_slice_in_dim(kp[b], ob * MOBA_BLOCK, MOBA_BLOCK, 0)
        v_own = lax.dynamic_slice_in_dim(vp[b], ob * MOBA_BLOCK, MOBA_BLOCK, 0)
        t = q0 + jnp.arange(Q_BLOCK)
        kpos = ob * MOBA_BLOCK + jnp.arange(MOBA_BLOCK)
        own_mask = kpos[None, :] <= t[:, None]
        return moba_core(qc, means[b], ob, kb[b], vb[b], k_own, v_own, own_mask)

    o = lax.map(step, jnp.arange(B * n_chunks))
    return o.reshape(B, S, N_HEADS, HEAD_DIM)


def moba_sample(q, k_new, v_new, cache_k, cache_v, page_table):
    T = q.shape[1]
    nb_past = PAST_LEN // MOBA_BLOCK
    own_start = nb_past * MOBA_BLOCK
    tail = PAST_LEN - own_start
    nbs = max(nb_past, MOBA_TOP_K)
    own_mask = np.concatenate([np.ones((T, tail), bool), np.tril(np.ones((T, T), bool))], axis=1)

    def step(args):
        qb, kn, vn, pt = args
        kpast = cache_k[pt].reshape(PAST_LEN, N_KV_HEADS, HEAD_DIM)
        vpast = cache_v[pt].reshape(PAST_LEN, N_KV_HEADS, HEAD_DIM)
        kr = kpast[:own_start].reshape(nb_past, MOBA_BLOCK, N_KV_HEADS, HEAD_DIM)
        vr = vpast[:own_start].reshape(nb_past, MOBA_BLOCK, N_KV_HEADS, HEAD_DIM)
        means = jnp.pad(jnp.mean(kr.astype(jnp.float32), axis=1), ((0, nbs - nb_past), (0, 0), (0, 0)))
        bpad = ((0, 0), (0, nbs - nb_past), (0, 0), (0, 0))
        kb = jnp.pad(kr.transpose(2, 0, 1, 3), bpad)
        vb = jnp.pad(vr.transpose(2, 0, 1, 3), bpad)
        k_own = jnp.concatenate([kpast[own_start:], kn.astype(kpast.dtype)], axis=0)
        v_own = jnp.concatenate([vpast[own_start:], vn.astype(vpast.dtype)], axis=0)
        return moba_core(qb, means, nb_past, kb, vb, k_own, v_own, own_mask)

    return lax.map(step, (q, k_new, v_new, page_table))


def trunk_layer(x, pos, conv_prev, attend, norm1_g, w_in, q_norm_g, k_norm_g, conv_dw_w, conv_dw_b,
                conv_ln_g, conv_ln_b, w_conv_out, w_out, norm2_g, w_ffn_in, w_ffn_out):
    B, L, _ = x.shape
    h = rms_norm(x, norm1_g)
    q, k, v, ca, cb, gc, ga = jnp.split(h @ w_in, IN_SPLITS, axis=-1)
    q = partial_rope(rms_norm(q.reshape(B, L, N_HEADS, HEAD_DIM), q_norm_g), pos)
    k = partial_rope(rms_norm(k.reshape(B, L, N_KV_HEADS, HEAD_DIM), k_norm_g), pos)
    v = v.reshape(B, L, N_KV_HEADS, HEAD_DIM)
    o_attn = attend(q, k, v).reshape(B, L, N_HEADS * HEAD_DIM)
    u = ca * jax.nn.sigmoid(cb)
    u_ext = jnp.concatenate([conv_prev.astype(u.dtype), u], axis=1)
    yc = lax.conv_general_dilated(u_ext, conv_dw_w[:, None, :], window_strides=(1,), padding='VALID',
                                  dimension_numbers=('NWC', 'WIO', 'NWC'), feature_group_count=CONV_CH)
    yc = jax.nn.silu(layer_norm(yc + conv_dw_b, conv_ln_g, conv_ln_b)) @ w_conv_out
    m = jax.nn.sigmoid(gc) * yc + jax.nn.sigmoid(ga) * o_attn
    x1 = x + m @ w_out
    a, b = jnp.split(rms_norm(x1, norm2_g) @ w_ffn_in, 2, axis=-1)
    y = x1 + (jax.nn.silu(a) * b) @ w_ffn_out
    return y, k, v, u_ext[:, -CONV_STATE:]


def setup_inputs(seed: int = 0) -> dict:
    key = jax.random.key(seed)
    ks = jax.random.split(key, 24)
    n_pages = PAST_LEN // PAGE_SIZE
    n_phys = int(math.ceil(POOL_FACTOR * DEC_BATCH * n_pages))
    f32 = jnp.float32
    nrm = lambda k, shape, s: jax.random.normal(k, shape, f32) * s
    page_table = jax.random.permutation(ks[5], n_phys)[:DEC_BATCH * n_pages].reshape(DEC_BATCH, n_pages).astype(jnp.int32)
    return {
        'x_prompt': nrm(ks[0], (BATCH, SEQ, D_MODEL), 1.0),
        'x_sample': nrm(ks[1], (DEC_BATCH, DEC_SEQ, D_MODEL), 1.0),
        'cache_k': nrm(ks[2], (n_phys, PAGE_SIZE, N_KV_HEADS, HEAD_DIM), 1.0),
        'cache_v': nrm(ks[3], (n_phys, PAGE_SIZE, N_KV_HEADS, HEAD_DIM), 1.0),
        'state_conv': nrm(ks[4], (DEC_BATCH, CONV_STATE, CONV_CH), 0.5),
        'page_table': page_table,
        'norm1_g': 1.0 + nrm(ks[6], (D_MODEL,), 0.02),
        'w_in': nrm(ks[7], (D_MODEL, IN_W), D_MODEL ** -0.5),
        'q_norm_g': 1.0 + nrm(ks[8], (HEAD_DIM,), 0.02),
        'k_norm_g': 1.0 + nrm(ks[9], (HEAD_DIM,), 0.02),
        'conv_dw_w': nrm(ks[10], (CONV_WIDTH, CONV_CH), CONV_WIDTH ** -0.5),
        'conv_dw_b': nrm(ks[11], (CONV_CH,), 0.02),
        'conv_ln_g': 1.0 + nrm(ks[12], (CONV_CH,), 0.02),
        'conv_ln_b': nrm(ks[13], (CONV_CH,), 0.02),
        'w_conv_out': nrm(ks[14], (CONV_CH, D_MODEL), CONV_CH ** -0.5),
        'w_out': nrm(ks[15], (D_MODEL, D_MODEL), D_MODEL ** -0.5),
        'norm2_g': 1.0 + nrm(ks[16], (D_MODEL,), 0.02),
        'w_ffn_in': nrm(ks[17], (D_MODEL, 2 * FFN_HIDDEN), D_MODEL ** -0.5),
        'w_ffn_out': nrm(ks[18], (FFN_HIDDEN, D_MODEL), FFN_HIDDEN ** -0.5),
    }


def reference(x_prompt, x_sample, cache_k, cache_v, state_conv, page_table, norm1_g, w_in, q_norm_g,
              k_norm_g, conv_dw_w, conv_dw_b, conv_ln_g, conv_ln_b, w_conv_out, w_out, norm2_g,
              w_ffn_in, w_ffn_out):
    pos_prompt = jnp.arange(x_prompt.shape[1])
    pos_sample = PAST_LEN + jnp.arange(x_sample.shape[1])
    conv_zero = jnp.zeros((x_prompt.shape[0], CONV_STATE, CONV_CH), x_prompt.dtype)
    attend_sample = lambda q, k, v: moba_sample(q, k, v, cache_k, cache_v, page_table)
    y_prompt, y_sample = x_prompt, x_sample
    for _ in range(DEPTH):
        y_prompt, k_prompt, v_prompt, conv_prompt = trunk_layer(
            y_prompt, pos_prompt, conv_zero, moba_prompt, norm1_g, w_in, q_norm_g, k_norm_g,
            conv_dw_w, conv_dw_b, conv_ln_g, conv_ln_b, w_conv_out, w_out, norm2_g, w_ffn_in, w_ffn_out)
        y_sample, k_sample, v_sample, conv_sample = trunk_layer(
            y_sample, pos_sample, state_conv, attend_sample, norm1_g, w_in, q_norm_g, k_norm_g,
            conv_dw_w, conv_dw_b, conv_ln_g, conv_ln_b, w_conv_out, w_out, norm2_g, w_ffn_in, w_ffn_out)
    return (y_prompt, y_sample, k_prompt, v_prompt, conv_prompt, k_sample, v_sample, conv_sample)
```

```python
import functools

import jax
import jax.numpy as jnp
from jax import lax
from jax.experimental import pallas as pl
from jax.experimental.pallas import tpu as pltpu

F32 = jnp.float32
BF16 = jnp.bfloat16

D_MODEL = 1024
N_HEADS = 8
N_KV_HEADS = 4
HEAD_DIM = 128
KV_GROUP = N_HEADS // N_KV_HEADS
ROT_DIM = HEAD_DIM // 4
ROPE_THETA = 500000.0
MOBA_BLOCK = 256
MOBA_TOP_K = 3
PAGE_SIZE = 128
CONV_WIDTH = 31
CONV_STATE = CONV_WIDTH - 1
FFN_HIDDEN = 2816
NORM_EPS = 1e-6
NEG_INF = -1e30
Q_W = N_HEADS * HEAD_DIM
KV_W = N_KV_HEADS * HEAD_DIM
IN_W = Q_W + 2 * KV_W + 4 * D_MODEL
ATTN_SCALE = HEAD_DIM ** -0.5

VMEM_LIMIT_BYTES = 56 * 1024 * 1024

IN_TILE_ROWS = 512
POST_TILE_ROWS = 256
CONV_TILE_ROWS = 128
CONV_HALO_ROWS = 32
CONV_ROW_CHUNK = 64
SAMPLE_CHUNK_PAGES = 8
SAMPLE_RING = 4

NT_DIMS = (((1,), (1,)), ((), ()))


def _sigmoid(x):
    return 1.0 / (1.0 + jnp.exp(-x))


def _const_spec(shape):
    zeros = (0,) * len(shape)
    return pl.BlockSpec(shape, lambda *_: zeros, pipeline_mode=pl.Buffered(1))


def _in_proj_kernel(x_ref, g1_ref, w_ref, qg_ref, kg_ref, cos_ref, sa_ref, sb_ref,
                    q_ref, k32_ref, kbf_ref, v32_ref, vbf_ref, u_ref, sgc_ref, sga_ref, ksum_ref):
    x = x_ref[...]
    h = (x * lax.rsqrt(jnp.mean(x * x, axis=-1, keepdims=True) + NORM_EPS) * g1_ref[...]).astype(BF16)
    cos, sa, sb = cos_ref[...], sa_ref[...], sb_ref[...]
    half = ROT_DIM // 2

    def norm_rope(z, g):
        zn = z * lax.rsqrt(jnp.mean(z * z, axis=-1, keepdims=True) + NORM_EPS) * g
        return zn * cos + pltpu.roll(zn, HEAD_DIM - half, 1) * sa + pltpu.roll(zn, half, 1) * sb

    def proj(c0, width):
        return jnp.dot(h, w_ref[:, c0:c0 + width], preferred_element_type=F32)

    zq = proj(0, Q_W)
    qg = qg_ref[...]
    for hh in range(N_HEADS):
        sl = slice(hh * HEAD_DIM, (hh + 1) * HEAD_DIM)
        q_ref[:, sl] = (norm_rope(zq[:, sl], qg) * ATTN_SCALE).astype(BF16)

    zkv = proj(Q_W, 2 * KV_W)
    kg = kg_ref[...]
    n_blk = ksum_ref.shape[1]
    for g in range(N_KV_HEADS):
        sl = slice(g * HEAD_DIM, (g + 1) * HEAD_DIM)
        kk = norm_rope(zkv[:, sl], kg)
        k32_ref[:, sl] = kk
        kbf_ref[:, sl] = kk.astype(BF16)
        for blk in range(n_blk):
            ksum_ref[0, blk:blk + 1, sl] = jnp.sum(
                kk[blk * MOBA_BLOCK:(blk + 1) * MOBA_BLOCK], axis=0, keepdims=True)
    vv = zkv[:, KV_W:]
    v32_ref[...] = vv
    vbf_ref[...] = vv.astype(BF16)

    c0 = Q_W + 2 * KV_W
    u_ref[...] = proj(c0, D_MODEL) * _sigmoid(proj(c0 + D_MODEL, D_MODEL))
    sgc_ref[...] = _sigmoid(proj(c0 + 2 * D_MODEL, D_MODEL)).astype(BF16)
    sga_ref[...] = _sigmoid(proj(c0 + 3 * D_MODEL, D_MODEL)).astype(BF16)


def _in_proj(x2d, g1, w_bf, qg, kg, cos_t, sa_t, sb_t):
    m = x2d.shape[0]
    tm = IN_TILE_ROWS
    nt = m // tm
    ntab = cos_t.shape[0] // tm
    n_blk = tm // MOBA_BLOCK
    row = lambda i: (i, 0)
    tab = lambda i: (i % ntab, 0)
    rows = lambda w: pl.BlockSpec((tm, w), row)
    out_shape = (
        jax.ShapeDtypeStruct((m, Q_W), BF16),
        jax.ShapeDtypeStruct((m, KV_W), F32),
        jax.ShapeDtypeStruct((m, KV_W), BF16),
        jax.ShapeDtypeStruct((m, KV_W), F32),
        jax.ShapeDtypeStruct((m, KV_W), BF16),
        jax.ShapeDtypeStruct((m, D_MODEL), F32),
        jax.ShapeDtypeStruct((m, D_MODEL), BF16),
        jax.ShapeDtypeStruct((m, D_MODEL), BF16),
        jax.ShapeDtypeStruct((nt, n_blk, KV_W), F32),
    )
    return pl.pallas_call(
        _in_proj_kernel,
        out_shape=out_shape,
        grid=(nt,),
        in_specs=[rows(D_MODEL), _const_spec((1, D_MODEL)), _const_spec((D_MODEL, IN_W)),
                  _const_spec((1, HEAD_DIM)), _const_spec((1, HEAD_DIM)),
                  pl.BlockSpec((tm, HEAD_DIM), tab), pl.BlockSpec((tm, HEAD_DIM), tab),
                  pl.BlockSpec((tm, HEAD_DIM), tab)],
        out_specs=(rows(Q_W), rows(KV_W), rows(KV_W), rows(KV_W), rows(KV_W), rows(D_MODEL),
                   rows(D_MODEL), rows(D_MODEL), pl.BlockSpec((1, n_blk, KV_W), lambda i: (i, 0, 0))),
        compiler_params=pltpu.CompilerParams(dimension_semantics=("arbitrary",),
                                             vmem_limit_bytes=VMEM_LIMIT_BYTES),
        name="in_proj",
    )(x2d, g1, w_bf, qg, kg, cos_t, sa_t, sb_t)


def _split3_bf16(a):
    a1 = a.astype(BF16)
    r1 = a - a1.astype(F32)
    a2 = r1.astype(BF16)
    a3 = (r1 - a2.astype(F32)).astype(BF16)
    return a1, a2, a3


def _block_scores(q_bf, means):
    return sum(lax.dot_general(q_bf, part, NT_DIMS, preferred_element_type=F32)
               for part in _split3_bf16(means))


def _select_bias(scores, n_elig):
    nb = scores.shape[1]
    col = lax.broadcasted_iota(jnp.int32, scores.shape, 1)
    sc = jnp.where(col < n_elig, scores, NEG_INF)
    rank = jnp.zeros(scores.shape, F32)
    for c in range(nb):
        s_c = sc[:, c:c + 1]
        tie = jnp.where(col > c, 1.0, 0.0)
        rank = rank + jnp.where(s_c > sc, 1.0, jnp.where(s_c == sc, tie, 0.0))
    keep = jnp.where(col < n_elig, jnp.where(rank < MOBA_TOP_K, 1.0, 0.0), 0.0)
    return jnp.where(keep > 0.5, 0.0, NEG_INF)


def _attn_prompt_kernel(q_ref, k_ref, v_ref, ks_ref, o_ref):
    i = pl.program_id(2)
    tq = q_ref.shape[0]
    q2 = jnp.concatenate([q_ref[:, :HEAD_DIM], q_ref[:, HEAD_DIM:]], axis=0)
    rows = q2.shape[0]
    means = ks_ref[0] * (1.0 / MOBA_BLOCK)
    bias = _select_bias(_block_scores(q2, means), i)
    col = lax.broadcasted_iota(jnp.int32, bias.shape, 1)

    def update(carry, s, vj):
        m, l, acc = carry
        m_new = jnp.maximum(m, jnp.max(s, axis=-1, keepdims=True))
        alpha = jnp.exp(m - m_new)
        p = jnp.exp(s - m_new)
        l = alpha * l + jnp.sum(p, axis=-1, keepdims=True)
        acc = alpha * acc + jnp.dot(p.astype(BF16), vj, preferred_element_type=F32)
        return m_new, l, acc

    def kv_block(j):
        start = pl.multiple_of(j * MOBA_BLOCK, MOBA_BLOCK)
        return k_ref[pl.ds(start, MOBA_BLOCK), :], v_ref[pl.ds(start, MOBA_BLOCK), :]

    def past_block(j, carry):
        kj, vj = kv_block(j)
        s = lax.dot_general(q2, kj, NT_DIMS, preferred_element_type=F32)
        s = s + jnp.sum(jnp.where(col == j, bias, 0.0), axis=-1, keepdims=True)
        return update(carry, s, vj)

    init = (jnp.full((rows, 1), NEG_INF, F32), jnp.zeros((rows, 1), F32),
            jnp.zeros((rows, HEAD_DIM), F32))
    carry = lax.fori_loop(0, i, past_block, init)

    kj, vj = kv_block(i)
    s = lax.dot_general(q2, kj, NT_DIMS, preferred_element_type=F32)
    t = lax.broadcasted_iota(jnp.int32, s.shape, 0) % tq
    kpos = lax.broadcasted_iota(jnp.int32, s.shape, 1)
    s = jnp.where(kpos <= t, s, NEG_INF)
    _, l, acc = update(carry, s, vj)
    out = acc / l
    o_ref[:, :HEAD_DIM] = out[:tq].astype(o_ref.dtype)
    o_ref[:, HEAD_DIM:] = out[tq:].astype(o_ref.dtype)


def _attn_prompt(q_bf, k_bf, v_bf, ksum, batch, seq):
    nb = seq // MOBA_BLOCK
    gw = KV_GROUP * HEAD_DIM
    return pl.pallas_call(
        _attn_prompt_kernel,
        out_shape=jax.ShapeDtypeStruct(q_bf.shape, BF16),
        grid=(batch, N_KV_HEADS, nb),
        in_specs=[pl.BlockSpec((MOBA_BLOCK, gw), lambda b, g, i: (b * nb + i, g)),
                  pl.BlockSpec((seq, HEAD_DIM), lambda b, g, i: (b, g)),
                  pl.BlockSpec((seq, HEAD_DIM), lambda b, g, i: (b, g)),
                  pl.BlockSpec((1, nb, HEAD_DIM), lambda b, g, i: (b, 0, g))],
        out_specs=pl.BlockSpec((MOBA_BLOCK, gw), lambda b, g, i: (b * nb + i, g)),
        compiler_params=pltpu.CompilerParams(
            dimension_semantics=("arbitrary", "arbitrary", "arbitrary"),
            vmem_limit_bytes=VMEM_LIMIT_BYTES),
        name="attn_prompt",
    )(q_bf, k_bf, v_bf, ksum)


def _attn_sample_kernel(pt_ref, q_ref, kn_ref, vn_ref, ck_hbm, cv_hbm, o_ref,
                        ring, sems, logit_ref, sum_ref, *, n_seq, n_pages):
    s = pl.program_id(0)
    cp = SAMPLE_CHUNK_PAGES
    n_kchunks = n_pages // cp
    n_chunks = 2 * n_kchunks
    chunk_keys = cp * PAGE_SIZE
    blk_per_chunk = chunk_keys // MOBA_BLOCK
    n_blocks = n_pages * PAGE_SIZE // MOBA_BLOCK
    n_q = q_ref.shape[2]
    t_new = kn_ref.shape[2]

    def page_copy(seq, c, r):
        src = ck_hbm if c < n_kchunks else cv_hbm
        page = pt_ref[seq, (c % n_kchunks) * cp + r]
        slot = c % SAMPLE_RING
        return pltpu.make_async_copy(src.at[page], ring.at[slot, pl.ds(r * PAGE_SIZE, PAGE_SIZE)],
                                     sems.at[slot])

    def start_chunk(seq, c):
        for r in range(cp):
            page_copy(seq, c, r).start()

    def wait_chunk(seq, c):
        for r in range(cp):
            page_copy(seq, c, r).wait()

    depth = SAMPLE_RING - 1

    @pl.when(s == 0)
    def _():
        for c in range(depth):
            start_chunk(s, c)

    qs = [q_ref[0, g] for g in range(N_KV_HEADS)]
    probs = [None] * N_KV_HEADS
    denom = [None] * N_KV_HEADS
    accs = [jnp.zeros((n_q, HEAD_DIM), F32) for _ in range(N_KV_HEADS)]
    own_p = [None] * N_KV_HEADS

    for c in range(n_chunks):
        wait_chunk(s, c)
        nxt = c + depth
        if nxt < n_chunks:
            start_chunk(s, nxt)
        else:
            @pl.when(s + 1 < n_seq)
            def _():
                start_chunk(s + 1, nxt - n_chunks)
        slot = c % SAMPLE_RING

        if c < n_kchunks:
            for b in range(blk_per_chunk):
                blk = ring[slot, b * MOBA_BLOCK:(b + 1) * MOBA_BLOCK, :]
                row = c * blk_per_chunk + b
                sum_ref[row:row + 1, :] = jnp.sum(blk, axis=0, keepdims=True)
            for g in range(N_KV_HEADS):
                kc = ring[slot, :, g * HEAD_DIM:(g + 1) * HEAD_DIM].astype(BF16)
                logit_ref[g, :, c * chunk_keys:(c + 1) * chunk_keys] = lax.dot_general(
                    qs[g], kc, NT_DIMS, preferred_element_type=F32)
        else:
            cv = c - n_kchunks
            if cv == 0:
                for g in range(N_KV_HEADS):
                    means = sum_ref[:, g * HEAD_DIM:(g + 1) * HEAD_DIM] * (1.0 / MOBA_BLOCK)
                    bias = _select_bias(_block_scores(qs[g], means), n_blocks)
                    qf = qs[g].astype(F32)
                    kn = kn_ref[0, g]
                    trow = lax.broadcasted_iota(jnp.int32, (n_q, 1), 0) % t_new
                    own = []
                    for tk in range(t_new):
                        lo = jnp.sum(qf * kn[tk:tk + 1, :], axis=-1, keepdims=True)
                        own.append(jnp.where(trow >= tk, lo, NEG_INF))
                    mx = own[0]
                    for lo in own[1:]:
                        mx = jnp.maximum(mx, lo)
                    for b in range(n_blocks):
                        sl = slice(b * MOBA_BLOCK, (b + 1) * MOBA_BLOCK)
                        lb = logit_ref[g, :, sl] + bias[:, b:b + 1]
                        logit_ref[g, :, sl] = lb
                        mx = jnp.maximum(mx, jnp.max(lb, axis=-1, keepdims=True))
                    p = jnp.exp(logit_ref[g] - mx)
                    own_p[g] = [jnp.exp(lo - mx) for lo in own]
                    den = jnp.sum(p, axis=-1, keepdims=True)
                    for po in own_p[g]:
                        den = den + po
                    denom[g] = den
                    probs[g] = p.astype(BF16)
            for g in range(N_KV_HEADS):
                vc = ring[slot, :, g * HEAD_DIM:(g + 1) * HEAD_DIM].astype(BF16)
                pc = probs[g][:, cv * chunk_keys:(cv + 1) * chunk_keys]
                accs[g] = accs[g] + jnp.dot(pc, vc, preferred_element_type=F32)

    for g in range(N_KV_HEADS):
        vn = vn_ref[0, g]
        acc = accs[g]
        for tk in range(t_new):
            acc = acc + own_p[g][tk] * vn[tk:tk + 1, :]
        o_ref[0, g] = acc / denom[g]


def _attn_sample(page_table, q_s, k_new, v_new, cache_k2, cache_v2):
    n_seq, n_pages = page_table.shape
    n_q, t_new = q_s.shape[2], k_new.shape[2]
    chunk_keys = SAMPLE_CHUNK_PAGES * PAGE_SIZE
    per_seq = lambda shape: pl.BlockSpec((1,) + shape, lambda s, pt: (s, 0, 0, 0))
    grid_spec = pltpu.PrefetchScalarGridSpec(
        num_scalar_prefetch=1,
        grid=(n_seq,),
        in_specs=[per_seq((N_KV_HEADS, n_q, HEAD_DIM)), per_seq((N_KV_HEADS, t_new, HEAD_DIM)),
                  per_seq((N_KV_HEADS, t_new, HEAD_DIM)),
                  pl.BlockSpec(memory_space=pl.ANY), pl.BlockSpec(memory_space=pl.ANY)],
        out_specs=per_seq((N_KV_HEADS, n_q, HEAD_DIM)),
        scratch_shapes=[pltpu.VMEM((SAMPLE_RING, chunk_keys, KV_W), F32),
                        pltpu.SemaphoreType.DMA((SAMPLE_RING,)),
                        pltpu.VMEM((N_KV_HEADS, n_q, n_pages * PAGE_SIZE), F32),
                        pltpu.VMEM((n_pages * PAGE_SIZE // MOBA_BLOCK, KV_W), F32)],
    )
    return pl.pallas_call(
        functools.partial(_attn_sample_kernel, n_seq=n_seq, n_pages=n_pages),
        out_shape=jax.ShapeDtypeStruct(q_s.shape, F32),
        grid_spec=grid_spec,
        compiler_params=pltpu.CompilerParams(dimension_semantics=("arbitrary",),
                                             vmem_limit_bytes=VMEM_LIMIT_BYTES),
        name="attn_sample",
    )(page_table, q_s, k_new, v_new, cache_k2, cache_v2)


def _ln_swish(yc, g, b):
    xc = yc - jnp.mean(yc, axis=-1, keepdims=True)
    var = jnp.mean(xc * xc, axis=-1, keepdims=True)
    z = xc * lax.rsqrt(var + NORM_EPS) * g + b
    return z * _sigmoid(z)


def _conv_prompt_kernel(u_ref, halo_ref, w_ref, cb_ref, lg_ref, lb_ref, o_ref, ext_ref, yc_ref,
                        *, tiles_per_seq):
    i = pl.program_id(0)
    tm = u_ref.shape[0]
    halo = halo_ref[...]
    ext_ref[0:CONV_HALO_ROWS, :] = jnp.where(i % tiles_per_seq == 0, jnp.zeros_like(halo), halo)
    ext_ref[CONV_HALO_ROWS:, :] = u_ref[...]
    lead = CONV_HALO_ROWS - CONV_STATE
    for rc in range(tm // CONV_ROW_CHUNK):
        for cc in range(D_MODEL // HEAD_DIM):
            lanes = slice(cc * HEAD_DIM, (cc + 1) * HEAD_DIM)
            acc = jnp.broadcast_to(cb_ref[:, lanes], (CONV_ROW_CHUNK, HEAD_DIM))
            for j in range(CONV_WIDTH):
                r0 = rc * CONV_ROW_CHUNK + lead + j
                acc = acc + w_ref[j:j + 1, lanes] * ext_ref[r0:r0 + CONV_ROW_CHUNK, lanes]
            yc_ref[rc * CONV_ROW_CHUNK:(rc + 1) * CONV_ROW_CHUNK, lanes] = acc
    o_ref[...] = _ln_swish(yc_ref[...], lg_ref[...], lb_ref[...]).astype(o_ref.dtype)


def _conv_prompt(u, conv_w, conv_b, ln_g, ln_b, seq):
    m = u.shape[0]
    tm = CONV_TILE_ROWS
    halo_per_tile = tm // CONV_HALO_ROWS
    return pl.pallas_call(
        functools.partial(_conv_prompt_kernel, tiles_per_seq=seq // tm),
        out_shape=jax.ShapeDtypeStruct((m, D_MODEL), BF16),
        grid=(m // tm,),
        in_specs=[pl.BlockSpec((tm, D_MODEL), lambda i: (i, 0)),
                  pl.BlockSpec((CONV_HALO_ROWS, D_MODEL),
                               lambda i: (jnp.maximum(i * halo_per_tile - 1, 0), 0)),
                  _const_spec((CONV_WIDTH, D_MODEL)), _const_spec((1, D_MODEL)),
                  _const_spec((1, D_MODEL)), _const_spec((1, D_MODEL))],
        out_specs=pl.BlockSpec((tm, D_MODEL), lambda i: (i, 0)),
        scratch_shapes=[pltpu.VMEM((tm + CONV_HALO_ROWS, D_MODEL), F32),
                        pltpu.VMEM((tm, D_MODEL), F32)],
        compiler_params=pltpu.CompilerParams(dimension_semantics=("arbitrary",),
                                             vmem_limit_bytes=VMEM_LIMIT_BYTES),
        name="conv_prompt",
    )(u, u, conv_w, conv_b, ln_g, ln_b)


def _conv_sample_kernel(st_ref, u0_ref, u1_ref, u2_ref, u3_ref, wst_ref, wnew_ref, cb_ref, lg_ref,
                        lb_ref, o0_ref, o1_ref, o2_ref, o3_ref):
    st = st_ref[...]
    new = [u0_ref[...], u1_ref[...], u2_ref[...], u3_ref[...]]
    outs = [o0_ref, o1_ref, o2_ref, o3_ref]
    for t in range(len(new)):
        yc = jnp.sum(st * wst_ref[t][None, :, :], axis=1) + cb_ref[...]
        for tk in range(t + 1):
            yc = yc + new[tk] * wnew_ref[t, tk:tk + 1, :]
        outs[t][...] = _ln_swish(yc, lg_ref[...], lb_ref[...]).astype(outs[t].dtype)


def _conv_sample(state, u_new, conv_w, conv_b, ln_g, ln_b):
    n_seq = state.shape[0]
    t_new = len(u_new)
    ts = 16
    w_state = jnp.stack([jnp.concatenate([jnp.zeros((t, D_MODEL), F32), conv_w[:CONV_STATE - t]], axis=0)
                         for t in range(t_new)])
    w_new = jnp.stack([jnp.concatenate([conv_w[CONV_STATE - t:], jnp.zeros((t_new - 1 - t, D_MODEL), F32)],
                                       axis=0) for t in range(t_new)])
    seqs = pl.BlockSpec((ts, D_MODEL), lambda i: (i, 0))
    return pl.pallas_call(
        _conv_sample_kernel,
        out_shape=tuple(jax.ShapeDtypeStruct((n_seq, D_MODEL), BF16) for _ in range(t_new)),
        grid=(n_seq // ts,),
        in_specs=[pl.BlockSpec((ts, CONV_STATE, D_MODEL), lambda i: (i, 0, 0))] + [seqs] * t_new + [
            _const_spec((t_new, CONV_STATE, D_MODEL)), _const_spec((t_new, t_new, D_MODEL)),
            _const_spec((1, D_MODEL)), _const_spec((1, D_MODEL)), _const_spec((1, D_MODEL))],
        out_specs=tuple(seqs for _ in range(t_new)),
        compiler_params=pltpu.CompilerParams(dimension_semantics=("arbitrary",),
                                             vmem_limit_bytes=VMEM_LIMIT_BYTES),
        name="conv_sample",
    )(state, *u_new, w_state, w_new, conv_b, ln_g, ln_b)


def _post_kernel(c_ref, sgc_ref, sga_ref, o_ref, x_ref, wco_ref, wo_ref, g2_ref, wfi_ref, wfo_ref,
                 y_ref):
    yc = jnp.dot(c_ref[...], wco_ref[...], preferred_element_type=F32)
    m = sgc_ref[...].astype(F32) * yc + sga_ref[...].astype(F32) * o_ref[...].astype(F32)
    x1 = x_ref[...] + jnp.dot(m.astype(BF16), wo_ref[...], preferred_element_type=F32)
    h2 = (x1 * lax.rsqrt(jnp.mean(x1 * x1, axis=-1, keepdims=True) + NORM_EPS) * g2_ref[...]).astype(BF16)
    ab = jnp.dot(h2, wfi_ref[...], preferred_element_type=F32)
    a, b = ab[:, :FFN_HIDDEN], ab[:, FFN_HIDDEN:]
    hid = (a * _sigmoid(a) * b).astype(BF16)
    y_ref[...] = x1 + jnp.dot(hid, wfo_ref[...], preferred_element_type=F32)


def _post(c_act, sgc, sga, o_attn, x2d, wco_bf, wo_bf, g2, wfi_bf, wfo_bf):
    m = x2d.shape[0]
    tm = POST_TILE_ROWS
    rows = pl.BlockSpec((tm, D_MODEL), lambda i: (i, 0))
    return pl.pallas_call(
        _post_kernel,
        out_shape=jax.ShapeDtypeStruct((m, D_MODEL), F32),
        grid=(m // tm,),
        in_specs=[rows, rows, rows, rows, rows,
                  _const_spec((D_MODEL, D_MODEL)), _const_spec((D_MODEL, D_MODEL)),
                  _const_spec((1, D_MODEL)), _const_spec((D_MODEL, 2 * FFN_HIDDEN)),
                  _const_spec((FFN_HIDDEN, D_MODEL))],
        out_specs=rows,
        compiler_params=pltpu.CompilerParams(dimension_semantics=("arbitrary",),
                                             vmem_limit_bytes=VMEM_LIMIT_BYTES),
        name="post",
    )(c_act, sgc, sga, o_attn, x2d, wco_bf, wo_bf, g2, wfi_bf, wfo_bf)


def _rope_tables(pos):
    half = ROT_DIM // 2
    inv_freq = ROPE_THETA ** (-(jnp.arange(half, dtype=F32) * 2.0 / ROT_DIM))
    ang = pos.astype(F32)[:, None] * inv_freq[None, :]
    cos, sin = jnp.cos(ang), jnp.sin(ang)
    n = pos.shape[0]
    zeros = lambda w: jnp.zeros((n, w), F32)
    cos_t = jnp.concatenate([cos, cos, jnp.ones((n, HEAD_DIM - ROT_DIM), F32)], axis=1)
    sa_t = jnp.concatenate([-sin, zeros(HEAD_DIM - half)], axis=1)
    sb_t = jnp.concatenate([zeros(half), sin, zeros(HEAD_DIM - ROT_DIM)], axis=1)
    return cos_t, sa_t, sb_t


def kernel(x_prompt, x_sample, cache_k, cache_v, state_conv, page_table, norm1_g, w_in, q_norm_g,
           k_norm_g, conv_dw_w, conv_dw_b, conv_ln_g, conv_ln_b, w_conv_out, w_out, norm2_g,
           w_ffn_in, w_ffn_out):
    batch, seq, _ = x_prompt.shape
    n_seq, t_new, _ = x_sample.shape
    n_phys = cache_k.shape[0]
    past_len = page_table.shape[1] * PAGE_SIZE
    assert seq % IN_TILE_ROWS == 0 and (n_seq * t_new) % IN_TILE_ROWS == 0
    assert IN_TILE_ROWS % t_new == 0 and past_len % MOBA_BLOCK == 0 and seq >= CONV_STATE

    row = lambda v: v.reshape(1, -1).astype(F32)
    w_in_bf, wco_bf, wo_bf = w_in.astype(BF16), w_conv_out.astype(BF16), w_out.astype(BF16)
    wfi_bf, wfo_bf = w_ffn_in.astype(BF16), w_ffn_out.astype(BF16)
    g1, g2, qg, kg = row(norm1_g), row(norm2_g), row(q_norm_g), row(k_norm_g)
    conv_b, ln_g, ln_b = row(conv_dw_b), row(conv_ln_g), row(conv_ln_b)

    xp = x_prompt.reshape(batch * seq, D_MODEL)
    q_p, k_p, kbf_p, v_p, vbf_p, u_p, sgc_p, sga_p, ksum_p = _in_proj(
        xp, g1, w_in_bf, qg, kg, *_rope_tables(jnp.arange(seq)))
    nb = seq // MOBA_BLOCK
    o_p = _attn_prompt(q_p, kbf_p, vbf_p, ksum_p.reshape(batch, nb, KV_W), batch, seq)
    c_p = _conv_prompt(u_p, conv_dw_w, conv_b, ln_g, ln_b, seq)
    y_p = _post(c_p, sgc_p, sga_p, o_p, xp, wco_bf, wo_bf, g2, wfi_bf, wfo_bf)

    xs = x_sample.reshape(n_seq * t_new, D_MODEL)
    pos_s = past_len + jnp.arange(IN_TILE_ROWS) % t_new
    q_s, k_s, _, v_s, _, u_s, sgc_s, sga_s, _ = _in_proj(
        xs, g1, w_in_bf, qg, kg, *_rope_tables(pos_s))
    q_g = q_s.reshape(n_seq, t_new, N_KV_HEADS, KV_GROUP, HEAD_DIM).transpose(0, 2, 3, 1, 4)
    q_g = q_g.reshape(n_seq, N_KV_HEADS, KV_GROUP * t_new, HEAD_DIM)
    k_new = k_s.reshape(n_seq, t_new, N_KV_HEADS, HEAD_DIM).transpose(0, 2, 1, 3)
    v_new = v_s.reshape(n_seq, t_new, N_KV_HEADS, HEAD_DIM).transpose(0, 2, 1, 3)
    o_g = _attn_sample(page_table, q_g, k_new, v_new,
                       cache_k.reshape(n_phys, PAGE_SIZE, KV_W), cache_v.reshape(n_phys, PAGE_SIZE, KV_W))
    o_s = o_g.reshape(n_seq, N_KV_HEADS, KV_GROUP, t_new, HEAD_DIM).transpose(0, 3, 1, 2, 4)
    o_s = o_s.reshape(n_seq * t_new, D_MODEL).astype(BF16)
    u_s3 = u_s.reshape(n_seq, t_new, D_MODEL)
    c_steps = _conv_sample(state_conv, [u_s3[:, t] for t in range(t_new)], conv_dw_w, conv_b, ln_g, ln_b)
    c_s = jnp.stack(c_steps, axis=1).reshape(n_seq * t_new, D_MODEL)
    y_s = _post(c_s, sgc_s, sga_s, o_s, xs, wco_bf, wo_bf, g2, wfi_bf, wfo_bf)

    kv_shape_p = (batch, seq, N_KV_HEADS, HEAD_DIM)
    kv_shape_s = (n_seq, t_new, N_KV_HEADS, HEAD_DIM)
    conv_prompt_state = u_p.reshape(batch, seq, D_MODEL)[:, seq - CONV_STATE:]
    conv_sample_state = jnp.concatenate([state_conv, u_s3], axis=1)[:, t_new:]
    return (y_p.reshape(batch, seq, D_MODEL), y_s.reshape(n_seq, t_new, D_MODEL),
            k_p.reshape(kv_shape_p), v_p.reshape(kv_shape_p), conv_prompt_state,
            k_s.reshape(kv_shape_s), v_s.reshape(kv_shape_s), conv_sample_state)
```

```python
import functools

import jax
import jax.numpy as jnp
from jax import lax
from jax.experimental import pallas as pl
from jax.experimental.pallas import tpu as pltpu

F32 = jnp.float32
BF16 = jnp.bfloat16

D_MODEL = 1024
N_HEADS = 8
N_KV_HEADS = 4
HEAD_DIM = 128
KV_GROUP = N_HEADS // N_KV_HEADS
ROT_DIM = HEAD_DIM // 4
ROPE_THETA = 500000.0
MOBA_BLOCK = 256
MOBA_SHIFT = MOBA_BLOCK.bit_length() - 1
ATTN_BLOCK_GROUPS = (4, 2, 1)
MOBA_TOP_K = 3
PAGE_SIZE = 128
CONV_WIDTH = 31
CONV_STATE = CONV_WIDTH - 1
FFN_HIDDEN = 2816
NORM_EPS = 1e-6
NEG_INF = -1e30
Q_W = N_HEADS * HEAD_DIM
KV_W = N_KV_HEADS * HEAD_DIM
IN_W = Q_W + 2 * KV_W + 4 * D_MODEL
ATTN_SCALE = HEAD_DIM ** -0.5
LOG2_E = 1.4426950408889634
Q_PRESCALE = ATTN_SCALE * LOG2_E

VMEM_LIMIT_BYTES = 56 * 1024 * 1024

IN_TILE_ROWS = 512
POST_TILE_ROWS = 256
CONV_TILE_ROWS = 128
CONV_HALO_ROWS = 32
CONV_ROW_CHUNK = 64
SUBLANES = 8
SAMPLE_CHUNK_PAGES = 8
SAMPLE_RING = 4

NT_DIMS = (((1,), (1,)), ((), ()))


def _sigmoid(x):
    return 1.0 / (1.0 + jnp.exp(-x))


def _const_spec(shape):
    zeros = (0,) * len(shape)
    return pl.BlockSpec(shape, lambda *_: zeros, pipeline_mode=pl.Buffered(1))


def _in_proj_kernel(x_ref, g1_ref, w_ref, qg_ref, kg_ref, cos_ref, sa_ref, sb_ref,
                    q_ref, k32_ref, kbf_ref, v32_ref, vbf_ref, u_ref, sgc_ref, sga_ref, ksum_ref):
    x = x_ref[...]
    h = (x * lax.rsqrt(jnp.mean(x * x, axis=-1, keepdims=True) + NORM_EPS) * g1_ref[...]).astype(BF16)
    cos, sa, sb = cos_ref[...], sa_ref[...], sb_ref[...]
    half = ROT_DIM // 2

    def norm_rope(z, g):
        zn = z * lax.rsqrt(jnp.mean(z * z, axis=-1, keepdims=True) + NORM_EPS) * g
        return zn * cos + pltpu.roll(zn, HEAD_DIM - half, 1) * sa + pltpu.roll(zn, half, 1) * sb

    def proj(c0, width):
        return jnp.dot(h, w_ref[:, c0:c0 + width], preferred_element_type=F32)

    zq = proj(0, Q_W)
    qg = qg_ref[...]
    for hh in range(N_HEADS):
        sl = slice(hh * HEAD_DIM, (hh + 1) * HEAD_DIM)
        q_ref[:, sl] = (norm_rope(zq[:, sl], qg) * Q_PRESCALE).astype(BF16)

    zkv = proj(Q_W, 2 * KV_W)
    kg = kg_ref[...]
    n_blk = ksum_ref.shape[1]
    for g in range(N_KV_HEADS):
        sl = slice(g * HEAD_DIM, (g + 1) * HEAD_DIM)
        kk = norm_rope(zkv[:, sl], kg)
        k32_ref[:, sl] = kk
        kbf_ref[:, sl] = kk.astype(BF16)
        for blk in range(n_blk):
            ksum_ref[0, blk:blk + 1, sl] = jnp.sum(
                kk[blk * MOBA_BLOCK:(blk + 1) * MOBA_BLOCK], axis=0, keepdims=True)
    vv = zkv[:, KV_W:]
    v32_ref[...] = vv
    vbf_ref[...] = vv.astype(BF16)

    c0 = Q_W + 2 * KV_W
    u_ref[...] = proj(c0, D_MODEL) * _sigmoid(proj(c0 + D_MODEL, D_MODEL))
    sgc_ref[...] = _sigmoid(proj(c0 + 2 * D_MODEL, D_MODEL)).astype(BF16)
    sga_ref[...] = _sigmoid(proj(c0 + 3 * D_MODEL, D_MODEL)).astype(BF16)


def _in_proj(x2d, g1, w_bf, qg, kg, cos_t, sa_t, sb_t):
    m = x2d.shape[0]
    tm = IN_TILE_ROWS
    nt = m // tm
    ntab = cos_t.shape[0] // tm
    n_blk = tm // MOBA_BLOCK
    row = lambda i: (i, 0)
    tab = lambda i: (i % ntab, 0)
    rows = lambda w: pl.BlockSpec((tm, w), row)
    out_shape = (
        jax.ShapeDtypeStruct((m, Q_W), BF16),
        jax.ShapeDtypeStruct((m, KV_W), F32),
        jax.ShapeDtypeStruct((m, KV_W), BF16),
        jax.ShapeDtypeStruct((m, KV_W), F32),
        jax.ShapeDtypeStruct((m, KV_W), BF16),
        jax.ShapeDtypeStruct((m, D_MODEL), F32),
        jax.ShapeDtypeStruct((m, D_MODEL), BF16),
        jax.ShapeDtypeStruct((m, D_MODEL), BF16),
        jax.ShapeDtypeStruct((nt, n_blk, KV_W), F32),
    )
    return pl.pallas_call(
        _in_proj_kernel,
        out_shape=out_shape,
        grid=(nt,),
        in_specs=[rows(D_MODEL), _const_spec((1, D_MODEL)), _const_spec((D_MODEL, IN_W)),
                  _const_spec((1, HEAD_DIM)), _const_spec((1, HEAD_DIM)),
                  pl.BlockSpec((tm, HEAD_DIM), tab), pl.BlockSpec((tm, HEAD_DIM), tab),
                  pl.BlockSpec((tm, HEAD_DIM), tab)],
        out_specs=(rows(Q_W), rows(KV_W), rows(KV_W), rows(KV_W), rows(KV_W), rows(D_MODEL),
                   rows(D_MODEL), rows(D_MODEL), pl.BlockSpec((1, n_blk, KV_W), lambda i: (i, 0, 0))),
        compiler_params=pltpu.CompilerParams(dimension_semantics=("arbitrary",),
                                             vmem_limit_bytes=VMEM_LIMIT_BYTES),
        name="in_proj",
    )(x2d, g1, w_bf, qg, kg, cos_t, sa_t, sb_t)


def _split3_bf16(a):
    a1 = a.astype(BF16)
    r1 = a - a1.astype(F32)
    a2 = r1.astype(BF16)
    a3 = (r1 - a2.astype(F32)).astype(BF16)
    return a1, a2, a3


def _block_scores(q_bf, means, blocks_first=False):
    if blocks_first:
        return sum(lax.dot_general(part, q_bf, NT_DIMS, preferred_element_type=F32)
                   for part in _split3_bf16(means))
    return sum(lax.dot_general(q_bf, part, NT_DIMS, preferred_element_type=F32)
               for part in _split3_bf16(means))


def _select_bias(scores, n_elig, block_axis):
    nb = scores.shape[block_axis]
    blk = lax.broadcasted_iota(jnp.int32, scores.shape, block_axis)
    sc = jnp.where(blk < n_elig, scores, NEG_INF)
    rank = jnp.zeros(scores.shape, F32)
    for c in range(nb):
        s_c = sc[c:c + 1, :] if block_axis == 0 else sc[:, c:c + 1]
        tie = jnp.where(blk > c, 1.0, 0.0)
        rank = rank + jnp.where(s_c > sc, 1.0, jnp.where(s_c == sc, tie, 0.0))
    keep = jnp.where(blk < n_elig, jnp.where(rank < MOBA_TOP_K, 1.0, 0.0), 0.0)
    return jnp.where(keep > 0.5, 0.0, NEG_INF)


def _attn_prompt_kernel(q_ref, k_ref, v_ref, ks_ref, o_ref, s_ref):
    i = pl.program_id(2)
    tq = q_ref.shape[0]
    nb = ks_ref.shape[1]
    q2 = jnp.concatenate([q_ref[:, :HEAD_DIM], q_ref[:, HEAD_DIM:]], axis=0)
    rows = q2.shape[0]
    means = ks_ref[0] * (1.0 / MOBA_BLOCK)
    bias_t = _select_bias(_block_scores(q2, means, blocks_first=True), i, 0)
    bias_pad = jnp.concatenate([bias_t, jnp.zeros((HEAD_DIM - nb, rows), F32)], axis=0)
    q_aug = jnp.concatenate([q2, bias_pad.T.astype(BF16)], axis=1)

    def block_start(j):
        return pl.multiple_of(j * MOBA_BLOCK, MOBA_BLOCK)

    def lane_max(mx, s):
        return jnp.maximum(mx, jnp.maximum(s[:, :HEAD_DIM], s[:, HEAD_DIM:]))

    def sweep(n, group_body, carry):
        done = 0
        for u in ATTN_BLOCK_GROUPS:
            count = (n - done) // u
            carry = lax.fori_loop(
                0, count, lambda t, c, done=done, u=u: group_body(done + t * u, u, c), carry)
            done = done + count * u
        return carry

    def past_logits(j0, u, mx):
        kk = k_ref[pl.ds(block_start(j0), u * MOBA_BLOCK), :]
        shape = (u * MOBA_BLOCK, HEAD_DIM)
        blk = j0 + (lax.broadcasted_iota(jnp.int32, shape, 0) >> MOBA_SHIFT)
        lane = lax.broadcasted_iota(jnp.int32, shape, 1)
        pick = jnp.where(lane == blk, 1.0, 0.0).astype(BF16)
        s = lax.dot_general(q_aug, jnp.concatenate([kk, pick], axis=1), NT_DIMS,
                            preferred_element_type=F32)
        for w in range(u):
            sw = s[:, w * MOBA_BLOCK:(w + 1) * MOBA_BLOCK]
            s_ref[j0 + w] = sw
            mx = lane_max(mx, sw)
        return mx

    mx = sweep(i, past_logits, jnp.full((rows, HEAD_DIM), NEG_INF, F32))

    ki = k_ref[pl.ds(block_start(i), MOBA_BLOCK), :]
    s = lax.dot_general(q2, ki, NT_DIMS, preferred_element_type=F32)
    t = lax.broadcasted_iota(jnp.int32, s.shape, 0) % tq
    kpos = lax.broadcasted_iota(jnp.int32, s.shape, 1)
    s = jnp.where(kpos <= t, s, NEG_INF)
    s_ref[i] = s
    mx = lane_max(mx, s)
    m = jnp.broadcast_to(jnp.max(mx, axis=-1, keepdims=True), (rows, HEAD_DIM))

    def weighted_values(j0, u, acc):
        parts = []
        for w in range(u):
            sw = s_ref[j0 + w]
            parts += [jnp.exp2(sw[:, :HEAD_DIM] - m), jnp.exp2(sw[:, HEAD_DIM:] - m)]
        p = jnp.concatenate(parts, axis=1).astype(BF16)
        vv = v_ref[pl.ds(block_start(j0), u * MOBA_BLOCK), :]
        ones = jnp.ones((u * MOBA_BLOCK, HEAD_DIM), BF16)
        return acc + jnp.dot(p, jnp.concatenate([vv, ones], axis=1), preferred_element_type=F32)

    acc = sweep(i + 1, weighted_values, jnp.zeros((rows, 2 * HEAD_DIM), F32))
    out = acc[:, :HEAD_DIM] / acc[:, HEAD_DIM:]
    o_ref[:, :HEAD_DIM] = out[:tq].astype(o_ref.dtype)
    o_ref[:, HEAD_DIM:] = out[tq:].astype(o_ref.dtype)


def _attn_prompt(q_bf, k_bf, v_bf, ksum, batch, seq):
    nb = seq // MOBA_BLOCK
    gw = KV_GROUP * HEAD_DIM
    return pl.pallas_call(
        _attn_prompt_kernel,
        out_shape=jax.ShapeDtypeStruct(q_bf.shape, BF16),
        grid=(batch, N_KV_HEADS, nb),
        in_specs=[pl.BlockSpec((MOBA_BLOCK, gw), lambda b, g, i: (b * nb + i, g)),
                  pl.BlockSpec((seq, HEAD_DIM), lambda b, g, i: (b, g)),
                  pl.BlockSpec((seq, HEAD_DIM), lambda b, g, i: (b, g)),
                  pl.BlockSpec((1, nb, HEAD_DIM), lambda b, g, i: (b, 0, g))],
        out_specs=pl.BlockSpec((MOBA_BLOCK, gw), lambda b, g, i: (b * nb + i, g)),
        scratch_shapes=[pltpu.VMEM((nb, KV_GROUP * MOBA_BLOCK, 2 * HEAD_DIM), F32)],
        compiler_params=pltpu.CompilerParams(
            dimension_semantics=("arbitrary", "arbitrary", "arbitrary"),
            vmem_limit_bytes=VMEM_LIMIT_BYTES),
        name="attn_prompt",
    )(q_bf, k_bf, v_bf, ksum)


def _attn_sample_kernel(pt_ref, q_ref, kn_ref, vn_ref, ck_hbm, cv_hbm, o_ref,
                        ring, sems, logit_ref, sum_ref, *, n_seq, n_pages):
    s = pl.program_id(0)
    cp = SAMPLE_CHUNK_PAGES
    n_kchunks = n_pages // cp
    n_chunks = 2 * n_kchunks
    chunk_keys = cp * PAGE_SIZE
    blk_per_chunk = chunk_keys // MOBA_BLOCK
    n_blocks = n_pages * PAGE_SIZE // MOBA_BLOCK
    n_q = q_ref.shape[2]
    t_new = kn_ref.shape[2]
    page_rows = PAGE_SIZE * N_KV_HEADS

    def page_copy(seq, c, r):
        src = ck_hbm if c < n_kchunks else cv_hbm
        page = pt_ref[seq, (c % n_kchunks) * cp + r]
        slot = c % SAMPLE_RING
        return pltpu.make_async_copy(src.at[page], ring.at[slot, pl.ds(r * page_rows, page_rows)],
                                     sems.at[slot])

    def head_rows(slot, g):
        return ring.at[slot][pl.ds(g, chunk_keys, stride=N_KV_HEADS), :]

    def start_chunk(seq, c):
        for r in range(cp):
            page_copy(seq, c, r).start()

    def wait_chunk(seq, c):
        for r in range(cp):
            page_copy(seq, c, r).wait()

    depth = SAMPLE_RING - 1

    @pl.when(s == 0)
    def _():
        for c in range(depth):
            start_chunk(s, c)

    qs = [q_ref[0, g] for g in range(N_KV_HEADS)]
    probs = [None] * N_KV_HEADS
    denom = [None] * N_KV_HEADS
    accs = [jnp.zeros((n_q, HEAD_DIM), F32) for _ in range(N_KV_HEADS)]
    own_p = [None] * N_KV_HEADS

    for c in range(n_chunks):
        wait_chunk(s, c)
        nxt = c + depth
        if nxt < n_chunks:
            start_chunk(s, nxt)
        else:
            @pl.when(s + 1 < n_seq)
            def _():
                start_chunk(s + 1, nxt - n_chunks)
        slot = c % SAMPLE_RING

        if c < n_kchunks:
            for g in range(N_KV_HEADS):
                kc = head_rows(slot, g)
                for b in range(blk_per_chunk):
                    row = c * blk_per_chunk + b
                    sum_ref[row:row + 1, g * HEAD_DIM:(g + 1) * HEAD_DIM] = jnp.sum(
                        kc[b * MOBA_BLOCK:(b + 1) * MOBA_BLOCK], axis=0, keepdims=True)
                logit_ref[g, :, c * chunk_keys:(c + 1) * chunk_keys] = lax.dot_general(
                    qs[g], kc.astype(BF16), NT_DIMS, preferred_element_type=F32)
        else:
            cv = c - n_kchunks
            if cv == 0:
                scores = jnp.concatenate(
                    [_block_scores(qs[g], sum_ref[:, g * HEAD_DIM:(g + 1) * HEAD_DIM] * (1.0 / MOBA_BLOCK))
                     for g in range(N_KV_HEADS)], axis=0)
                bias_all = _select_bias(scores, n_blocks, 1)
                e_shape = (n_blocks, n_blocks * MOBA_BLOCK)
                expand = jnp.where(
                    (lax.broadcasted_iota(jnp.int32, e_shape, 1) >> MOBA_SHIFT)
                    == lax.broadcasted_iota(jnp.int32, e_shape, 0), 1.0, 0.0).astype(BF16)
                for g in range(N_KV_HEADS):
                    bias = jnp.dot(bias_all[g * n_q:(g + 1) * n_q].astype(BF16), expand,
                                   preferred_element_type=F32)
                    qf = qs[g].astype(F32)
                    kn = kn_ref[0, g]
                    trow = lax.broadcasted_iota(jnp.int32, (n_q, 1), 0) % t_new
                    own = []
                    for tk in range(t_new):
                        lo = jnp.sum(qf * kn[tk:tk + 1, :], axis=-1, keepdims=True)
                        own.append(jnp.where(trow >= tk, lo, NEG_INF))
                    mx = own[0]
                    for lo in own[1:]:
                        mx = jnp.maximum(mx, lo)
                    lm = logit_ref[g] + bias
                    mx = jnp.maximum(mx, jnp.max(lm, axis=-1, keepdims=True))
                    p = jnp.exp2(lm - mx)
                    own_p[g] = [jnp.exp2(lo - mx) for lo in own]
                    den = jnp.sum(p, axis=-1, keepdims=True)
                    for po in own_p[g]:
                        den = den + po
                    denom[g] = den
                    probs[g] = p.astype(BF16)
            for g in range(N_KV_HEADS):
                vc = head_rows(slot, g).astype(BF16)
                pc = probs[g][:, cv * chunk_keys:(cv + 1) * chunk_keys]
                accs[g] = accs[g] + jnp.dot(pc, vc, preferred_element_type=F32)

    for g in range(N_KV_HEADS):
        vn = vn_ref[0, g]
        acc = accs[g]
        for tk in range(t_new):
            acc = acc + own_p[g][tk] * vn[tk:tk + 1, :]
        o_ref[0, g] = acc / denom[g]


def _attn_sample(page_table, q_s, k_new, v_new, cache_k2, cache_v2):
    n_seq, n_pages = page_table.shape
    n_q, t_new = q_s.shape[2], k_new.shape[2]
    chunk_keys = SAMPLE_CHUNK_PAGES * PAGE_SIZE
    per_seq = lambda shape: pl.BlockSpec((1,) + shape, lambda s, pt: (s, 0, 0, 0))
    grid_spec = pltpu.PrefetchScalarGridSpec(
        num_scalar_prefetch=1,
        grid=(n_seq,),
        in_specs=[per_seq((N_KV_HEADS, n_q, HEAD_DIM)), per_seq((N_KV_HEADS, t_new, HEAD_DIM)),
                  per_seq((N_KV_HEADS, t_new, HEAD_DIM)),
                  pl.BlockSpec(memory_space=pl.ANY), pl.BlockSpec(memory_space=pl.ANY)],
        out_specs=per_seq((N_KV_HEADS, n_q, HEAD_DIM)),
        scratch_shapes=[pltpu.VMEM((SAMPLE_RING, chunk_keys * N_KV_HEADS, HEAD_DIM), F32),
                        pltpu.SemaphoreType.DMA((SAMPLE_RING,)),
                        pltpu.VMEM((N_KV_HEADS, n_q, n_pages * PAGE_SIZE), F32),
                        pltpu.VMEM((n_pages * PAGE_SIZE // MOBA_BLOCK, KV_W), F32)],
    )
    return pl.pallas_call(
        functools.partial(_attn_sample_kernel, n_seq=n_seq, n_pages=n_pages),
        out_shape=jax.ShapeDtypeStruct(q_s.shape, F32),
        grid_spec=grid_spec,
        compiler_params=pltpu.CompilerParams(dimension_semantics=("arbitrary",),
                                             vmem_limit_bytes=VMEM_LIMIT_BYTES),
        name="attn_sample",
    )(page_table, q_s, k_new, v_new, cache_k2, cache_v2)


def _ln_swish(yc, g, b):
    xc = yc - jnp.mean(yc, axis=-1, keepdims=True)
    var = jnp.mean(xc * xc, axis=-1, keepdims=True)
    z = xc * lax.rsqrt(var + NORM_EPS) * g + b
    return z * _sigmoid(z)


def _conv_prompt_kernel(u_ref, halo_ref, w_ref, cb_ref, lg_ref, lb_ref, o_ref, ext_ref, sh_ref, yc_ref,
                        *, tiles_per_seq):
    i = pl.program_id(0)
    tm = u_ref.shape[0]
    halo = halo_ref[...]
    ext_ref[0:CONV_HALO_ROWS, :] = jnp.where(i % tiles_per_seq == 0, jnp.zeros_like(halo), halo)
    ext_ref[CONV_HALO_ROWS:, :] = u_ref[...]
    lead = CONV_HALO_ROWS - CONV_STATE
    sh_rows = sh_ref.shape[1]
    for ph in range(1, SUBLANES):
        sh_ref[ph - 1] = ext_ref[ph:ph + sh_rows, :]
    for rc in range(tm // CONV_ROW_CHUNK):
        for cc in range(D_MODEL // HEAD_DIM):
            lanes = slice(cc * HEAD_DIM, (cc + 1) * HEAD_DIM)
            acc = jnp.broadcast_to(cb_ref[:, lanes], (CONV_ROW_CHUNK, HEAD_DIM))
            for j in range(CONV_WIDTH):
                ph = (lead + j) % SUBLANES
                r0 = rc * CONV_ROW_CHUNK + (lead + j) - ph
                rows = (ext_ref[r0:r0 + CONV_ROW_CHUNK, lanes] if ph == 0
                        else sh_ref[ph - 1, r0:r0 + CONV_ROW_CHUNK, lanes])
                acc = acc + w_ref[j:j + 1, lanes] * rows
            yc_ref[rc * CONV_ROW_CHUNK:(rc + 1) * CONV_ROW_CHUNK, lanes] = acc
    o_ref[...] = _ln_swish(yc_ref[...], lg_ref[...], lb_ref[...]).astype(o_ref.dtype)


def _conv_prompt(u, conv_w, conv_b, ln_g, ln_b, seq):
    m = u.shape[0]
    tm = CONV_TILE_ROWS
    halo_per_tile = tm // CONV_HALO_ROWS
    return pl.pallas_call(
        functools.partial(_conv_prompt_kernel, tiles_per_seq=seq // tm),
        out_shape=jax.ShapeDtypeStruct((m, D_MODEL), BF16),
        grid=(m // tm,),
        in_specs=[pl.BlockSpec((tm, D_MODEL), lambda i: (i, 0)),
                  pl.BlockSpec((CONV_HALO_ROWS, D_MODEL),
                               lambda i: (jnp.maximum(i * halo_per_tile - 1, 0), 0)),
                  _const_spec((CONV_WIDTH, D_MODEL)), _const_spec((1, D_MODEL)),
                  _const_spec((1, D_MODEL)), _const_spec((1, D_MODEL))],
        out_specs=pl.BlockSpec((tm, D_MODEL), lambda i: (i, 0)),
        scratch_shapes=[pltpu.VMEM((tm + CONV_HALO_ROWS, D_MODEL), F32),
                        pltpu.VMEM((SUBLANES - 1, tm + CONV_HALO_ROWS - SUBLANES, D_MODEL), F32),
                        pltpu.VMEM((tm, D_MODEL), F32)],
        compiler_params=pltpu.CompilerParams(dimension_semantics=("arbitrary",),
                                             vmem_limit_bytes=VMEM_LIMIT_BYTES),
        name="conv_prompt",
    )(u, u, conv_w, conv_b, ln_g, ln_b)


def _conv_sample_kernel(st_ref, u0_ref, u1_ref, u2_ref, u3_ref, wst_ref, wnew_ref, cb_ref, lg_ref,
                        lb_ref, o0_ref, o1_ref, o2_ref, o3_ref):
    st = st_ref[...]
    new = [u0_ref[...], u1_ref[...], u2_ref[...], u3_ref[...]]
    outs = [o0_ref, o1_ref, o2_ref, o3_ref]
    for t in range(len(new)):
        yc = jnp.sum(st * wst_ref[t][None, :, :], axis=1) + cb_ref[...]
        for tk in range(t + 1):
            yc = yc + new[tk] * wnew_ref[t, tk:tk + 1, :]
        outs[t][...] = _ln_swish(yc, lg_ref[...], lb_ref[...]).astype(outs[t].dtype)


def _conv_sample(state, u_new, conv_w, conv_b, ln_g, ln_b):
    n_seq = state.shape[0]
    t_new = len(u_new)
    ts = 16
    w_state = jnp.stack([jnp.concatenate([jnp.zeros((t, D_MODEL), F32), conv_w[:CONV_STATE - t]], axis=0)
                         for t in range(t_new)])
    w_new = jnp.stack([jnp.concatenate([conv_w[CONV_STATE - t:], jnp.zeros((t_new - 1 - t, D_MODEL), F32)],
                                       axis=0) for t in range(t_new)])
    seqs = pl.BlockSpec((ts, D_MODEL), lambda i: (i, 0))
    return pl.pallas_call(
        _conv_sample_kernel,
        out_shape=tuple(jax.ShapeDtypeStruct((n_seq, D_MODEL), BF16) for _ in range(t_new)),
        grid=(n_seq // ts,),
        in_specs=[pl.BlockSpec((ts, CONV_STATE, D_MODEL), lambda i: (i, 0, 0))] + [seqs] * t_new + [
            _const_spec((t_new, CONV_STATE, D_MODEL)), _const_spec((t_new, t_new, D_MODEL)),
            _const_spec((1, D_MODEL)), _const_spec((1, D_MODEL)), _const_spec((1, D_MODEL))],
        out_specs=tuple(seqs for _ in range(t_new)),
        compiler_params=pltpu.CompilerParams(dimension_semantics=("arbitrary",),
                                             vmem_limit_bytes=VMEM_LIMIT_BYTES),
        name="conv_sample",
    )(state, *u_new, w_state, w_new, conv_b, ln_g, ln_b)


def _post_kernel(c_ref, sgc_ref, sga_ref, o_ref, x_ref, wco_ref, wo_ref, g2_ref, wfi_ref, wfo_ref,
                 y_ref):
    yc = jnp.dot(c_ref[...], wco_ref[...], preferred_element_type=F32)
    m = sgc_ref[...].astype(F32) * yc + sga_ref[...].astype(F32) * o_ref[...].astype(F32)
    x1 = x_ref[...] + jnp.dot(m.astype(BF16), wo_ref[...], preferred_element_type=F32)
    h2 = (x1 * lax.rsqrt(jnp.mean(x1 * x1, axis=-1, keepdims=True) + NORM_EPS) * g2_ref[...]).astype(BF16)
    ab = jnp.dot(h2, wfi_ref[...], preferred_element_type=F32)
    a, b = ab[:, :FFN_HIDDEN], ab[:, FFN_HIDDEN:]
    hid = (a * _sigmoid(a) * b).astype(BF16)
    y_ref[...] = x1 + jnp.dot(hid, wfo_ref[...], preferred_element_type=F32)


def _post(c_act, sgc, sga, o_attn, x2d, wco_bf, wo_bf, g2, wfi_bf, wfo_bf):
    m = x2d.shape[0]
    tm = POST_TILE_ROWS
    rows = pl.BlockSpec((tm, D_MODEL), lambda i: (i, 0))
    return pl.pallas_call(
        _post_kernel,
        out_shape=jax.ShapeDtypeStruct((m, D_MODEL), F32),
        grid=(m // tm,),
        in_specs=[rows, rows, rows, rows, rows,
                  _const_spec((D_MODEL, D_MODEL)), _const_spec((D_MODEL, D_MODEL)),
                  _const_spec((1, D_MODEL)), _const_spec((D_MODEL, 2 * FFN_HIDDEN)),
                  _const_spec((FFN_HIDDEN, D_MODEL))],
        out_specs=rows,
        compiler_params=pltpu.CompilerParams(dimension_semantics=("arbitrary",),
                                             vmem_limit_bytes=VMEM_LIMIT_BYTES),
        name="post",
    )(c_act, sgc, sga, o_attn, x2d, wco_bf, wo_bf, g2, wfi_bf, wfo_bf)


def _rope_tables(pos):
    half = ROT_DIM // 2
    inv_freq = ROPE_THETA ** (-(jnp.arange(half, dtype=F32) * 2.0 / ROT_DIM))
    ang = pos.astype(F32)[:, None] * inv_freq[None, :]
    cos, sin = jnp.cos(ang), jnp.sin(ang)
    n = pos.shape[0]
    zeros = lambda w: jnp.zeros((n, w), F32)
    cos_t = jnp.concatenate([cos, cos, jnp.ones((n, HEAD_DIM - ROT_DIM), F32)], axis=1)
    sa_t = jnp.concatenate([-sin, zeros(HEAD_DIM - half)], axis=1)
    sb_t = jnp.concatenate([zeros(half), sin, zeros(HEAD_DIM - ROT_DIM)], axis=1)
    return cos_t, sa_t, sb_t


def kernel(x_prompt, x_sample, cache_k, cache_v, state_conv, page_table, norm1_g, w_in, q_norm_g,
           k_norm_g, conv_dw_w, conv_dw_b, conv_ln_g, conv_ln_b, w_conv_out, w_out, norm2_g,
           w_ffn_in, w_ffn_out):
    batch, seq, _ = x_prompt.shape
    n_seq, t_new, _ = x_sample.shape
    n_phys = cache_k.shape[0]
    past_len = page_table.shape[1] * PAGE_SIZE
    assert seq % IN_TILE_ROWS == 0 and (n_seq * t_new) % IN_TILE_ROWS == 0
    assert IN_TILE_ROWS % t_new == 0 and past_len % MOBA_BLOCK == 0 and seq >= CONV_STATE

    row = lambda v: v.reshape(1, -1).astype(F32)
    w_in_bf, wco_bf, wo_bf = w_in.astype(BF16), w_conv_out.astype(BF16), w_out.astype(BF16)
    wfi_bf, wfo_bf = w_ffn_in.astype(BF16), w_ffn_out.astype(BF16)
    g1, g2, qg, kg = row(norm1_g), row(norm2_g), row(q_norm_g), row(k_norm_g)
    conv_b, ln_g, ln_b = row(conv_dw_b), row(conv_ln_g), row(conv_ln_b)

    xp = x_prompt.reshape(batch * seq, D_MODEL)
    q_p, k_p, kbf_p, v_p, vbf_p, u_p, sgc_p, sga_p, ksum_p = _in_proj(
        xp, g1, w_in_bf, qg, kg, *_rope_tables(jnp.arange(seq)))
    nb = seq // MOBA_BLOCK
    o_p = _attn_prompt(q_p, kbf_p, vbf_p, ksum_p.reshape(batch, nb, KV_W), batch, seq)
    c_p = _conv_prompt(u_p, conv_dw_w, conv_b, ln_g, ln_b, seq)
    y_p = _post(c_p, sgc_p, sga_p, o_p, xp, wco_bf, wo_bf, g2, wfi_bf, wfo_bf)

    xs = x_sample.reshape(n_seq * t_new, D_MODEL)
    pos_s = past_len + jnp.arange(IN_TILE_ROWS) % t_new
    q_s, k_s, _, v_s, _, u_s, sgc_s, sga_s, _ = _in_proj(
        xs, g1, w_in_bf, qg, kg, *_rope_tables(pos_s))
    q_g = q_s.reshape(n_seq, t_new, N_KV_HEADS, KV_GROUP, HEAD_DIM).transpose(0, 2, 3, 1, 4)
    q_g = q_g.reshape(n_seq, N_KV_HEADS, KV_GROUP * t_new, HEAD_DIM)
    k_new = k_s.reshape(n_seq, t_new, N_KV_HEADS, HEAD_DIM).transpose(0, 2, 1, 3)
    v_new = v_s.reshape(n_seq, t_new, N_KV_HEADS, HEAD_DIM).transpose(0, 2, 1, 3)
    page_rows = PAGE_SIZE * N_KV_HEADS
    o_g = _attn_sample(page_table, q_g, k_new, v_new, cache_k.reshape(n_phys, page_rows, HEAD_DIM),
                       cache_v.reshape(n_phys, page_rows, HEAD_DIM))
    o_s = o_g.reshape(n_seq, N_KV_HEADS, KV_GROUP, t_new, HEAD_DIM).transpose(0, 3, 1, 2, 4)
    o_s = o_s.reshape(n_seq * t_new, D_MODEL).astype(BF16)
    u_s3 = u_s.reshape(n_seq, t_new, D_MODEL)
    c_steps = _conv_sample(state_conv, [u_s3[:, t] for t in range(t_new)], conv_dw_w, conv_b, ln_g, ln_b)
    c_s = jnp.stack(c_steps, axis=1).reshape(n_seq * t_new, D_MODEL)
    y_s = _post(c_s, sgc_s, sga_s, o_s, xs, wco_bf, wo_bf, g2, wfi_bf, wfo_bf)

    kv_shape_p = (batch, seq, N_KV_HEADS, HEAD_DIM)
    kv_shape_s = (n_seq, t_new, N_KV_HEADS, HEAD_DIM)
    conv_prompt_state = u_p.reshape(batch, seq, D_MODEL)[:, seq - CONV_STATE:]
    conv_sample_state = jnp.concatenate([state_conv, u_s3], axis=1)[:, t_new:]
    return (y_p.reshape(batch, seq, D_MODEL), y_s.reshape(n_seq, t_new, D_MODEL),
            k_p.reshape(kv_shape_p), v_p.reshape(kv_shape_p), conv_prompt_state,
            k_s.reshape(kv_shape_s), v_s.reshape(kv_shape_s), conv_sample_state)
```

```python
import functools

import jax
import jax.numpy as jnp
from jax import lax
from jax.experimental import pallas as pl
from jax.experimental.pallas import tpu as pltpu

F32 = jnp.float32
BF16 = jnp.bfloat16

D_MODEL = 1024
N_HEADS = 8
N_KV_HEADS = 4
HEAD_DIM = 128
KV_GROUP = N_HEADS // N_KV_HEADS
ROT_DIM = HEAD_DIM // 4
ROPE_THETA = 500000.0
MOBA_BLOCK = 256
MOBA_SHIFT = MOBA_BLOCK.bit_length() - 1
ATTN_BLOCK_GROUPS = (8, 4, 2, 1)
MOBA_TOP_K = 3
PAGE_SIZE = 128
CONV_WIDTH = 31
CONV_STATE = CONV_WIDTH - 1
FFN_HIDDEN = 2816
NORM_EPS = 1e-6
NEG_INF = -1e30
Q_W = N_HEADS * HEAD_DIM
KV_W = N_KV_HEADS * HEAD_DIM
IN_W = Q_W + 2 * KV_W + 4 * D_MODEL
ATTN_SCALE = HEAD_DIM ** -0.5
LOG2_E = 1.4426950408889634
Q_PRESCALE = ATTN_SCALE * LOG2_E

VMEM_LIMIT_BYTES = 56 * 1024 * 1024

IN_TILE_ROWS = 512
POST_TILE_ROWS = 256
CONV_TILE_ROWS = 128
CONV_HALO_ROWS = 32
CONV_ROW_CHUNK = 64
SUBLANES = 8
SAMPLE_CHUNK_PAGES = 8
SAMPLE_RING = 4
SAMPLE_V_GROUP = 4

NT_DIMS = (((1,), (1,)), ((), ()))


def _sigmoid(x):
    return 1.0 / (1.0 + jnp.exp(-x))


def _const_spec(shape):
    zeros = (0,) * len(shape)
    return pl.BlockSpec(shape, lambda *_: zeros, pipeline_mode=pl.Buffered(1))


def _ln_swish(yc, g, b):
    xc = yc - jnp.mean(yc, axis=-1, keepdims=True)
    var = jnp.mean(xc * xc, axis=-1, keepdims=True)
    z = xc * lax.rsqrt(var + NORM_EPS) * g + b
    return z * _sigmoid(z)


def _causal_conv_tile(ext_ref, sh_ref, yc_ref, base, w_ref, cb_ref):
    lead = CONV_HALO_ROWS - CONV_STATE
    sh_rows = sh_ref.shape[1]
    for ph in range(1, SUBLANES):
        sh_ref[ph - 1] = ext_ref[base + ph:base + ph + sh_rows, :]
    for rc in range(CONV_TILE_ROWS // CONV_ROW_CHUNK):
        for cc in range(D_MODEL // HEAD_DIM):
            lanes = slice(cc * HEAD_DIM, (cc + 1) * HEAD_DIM)
            acc = jnp.broadcast_to(cb_ref[:, lanes], (CONV_ROW_CHUNK, HEAD_DIM))
            for j in range(CONV_WIDTH):
                ph = (lead + j) % SUBLANES
                r0 = rc * CONV_ROW_CHUNK + (lead + j) - ph
                rows = (ext_ref[base + r0:base + r0 + CONV_ROW_CHUNK, lanes] if ph == 0
                        else sh_ref[ph - 1, r0:r0 + CONV_ROW_CHUNK, lanes])
                acc = acc + w_ref[j:j + 1, lanes] * rows
            yc_ref[rc * CONV_ROW_CHUNK:(rc + 1) * CONV_ROW_CHUNK, lanes] = acc


def _in_proj_kernel(*refs, fuse_conv, tiles_per_seq):
    x_ref, g1_ref, w_ref, qg_ref, kg_ref, cos_ref, sa_ref, sb_ref = refs[:8]
    if fuse_conv:
        cw_ref, cb_ref, lg_ref, lb_ref = refs[8:12]
        (q_ref, k32_ref, kbf_ref, v32_ref, vbf_ref, c_ref, sgc_ref, sga_ref, ksum_ref, tail_out_ref,
         ext_ref, sh_ref, yc_ref) = refs[12:]
    else:
        q_ref, k32_ref, kbf_ref, v32_ref, vbf_ref, u_ref, sgc_ref, sga_ref, ksum_ref = refs[8:]
    tm = x_ref.shape[0]
    x = x_ref[...]
    h = (x * lax.rsqrt(jnp.mean(x * x, axis=-1, keepdims=True) + NORM_EPS) * g1_ref[...]).astype(BF16)
    cos, sa, sb = cos_ref[...], sa_ref[...], sb_ref[...]
    half = ROT_DIM // 2

    def norm_rope(z, g):
        zn = z * lax.rsqrt(jnp.mean(z * z, axis=-1, keepdims=True) + NORM_EPS) * g
        return zn * cos + pltpu.roll(zn, HEAD_DIM - half, 1) * sa + pltpu.roll(zn, half, 1) * sb

    def proj(c0, width):
        return jnp.dot(h, w_ref[:, c0:c0 + width], preferred_element_type=F32)

    c0 = Q_W + 2 * KV_W
    u = proj(c0, D_MODEL) * _sigmoid(proj(c0 + D_MODEL, D_MODEL))
    if fuse_conv:
        first = pl.program_id(0) % tiles_per_seq == 0

        @pl.when(first)
        def _():
            ext_ref[0:CONV_HALO_ROWS, :] = jnp.zeros((CONV_HALO_ROWS, D_MODEL), F32)

        @pl.when(jnp.logical_not(first))
        def _():
            ext_ref[0:CONV_HALO_ROWS, :] = ext_ref[tm:tm + CONV_HALO_ROWS, :]

        ext_ref[CONV_HALO_ROWS:, :] = u
        tail_out_ref[0] = u[tm - CONV_HALO_ROWS:]
        for sub in range(tm // CONV_TILE_ROWS):
            _causal_conv_tile(ext_ref, sh_ref, yc_ref, sub * CONV_TILE_ROWS, cw_ref, cb_ref)
            c_ref[sub * CONV_TILE_ROWS:(sub + 1) * CONV_TILE_ROWS, :] = _ln_swish(
                yc_ref[...], lg_ref[...], lb_ref[...]).astype(c_ref.dtype)
    else:
        u_ref[...] = u

    zq = proj(0, Q_W)
    qg = qg_ref[...]
    for hh in range(N_HEADS):
        sl = slice(hh * HEAD_DIM, (hh + 1) * HEAD_DIM)
        q_ref[:, sl] = (norm_rope(zq[:, sl], qg) * Q_PRESCALE).astype(BF16)

    zkv = proj(Q_W, 2 * KV_W)
    kg = kg_ref[...]
    n_blk = ksum_ref.shape[1]
    for g in range(N_KV_HEADS):
        sl = slice(g * HEAD_DIM, (g + 1) * HEAD_DIM)
        kk = norm_rope(zkv[:, sl], kg)
        vv = zkv[:, KV_W + g * HEAD_DIM:KV_W + (g + 1) * HEAD_DIM]
        k32_ref[pl.ds(g, tm, stride=N_KV_HEADS), :] = kk
        v32_ref[pl.ds(g, tm, stride=N_KV_HEADS), :] = vv
        kbf_ref[:, sl] = kk.astype(BF16)
        vbf_ref[:, sl] = vv.astype(BF16)
        for blk in range(n_blk):
            ksum_ref[0, blk:blk + 1, sl] = jnp.sum(
                kk[blk * MOBA_BLOCK:(blk + 1) * MOBA_BLOCK], axis=0, keepdims=True)

    sgc_ref[...] = _sigmoid(proj(c0 + 2 * D_MODEL, D_MODEL)).astype(BF16)
    sga_ref[...] = _sigmoid(proj(c0 + 3 * D_MODEL, D_MODEL)).astype(BF16)


def _in_proj(x2d, g1, w_bf, qg, kg, cos_t, sa_t, sb_t, conv=None, seq=None):
    m = x2d.shape[0]
    tm = IN_TILE_ROWS
    nt = m // tm
    ntab = cos_t.shape[0] // tm
    n_blk = tm // MOBA_BLOCK
    fuse_conv = conv is not None
    row = lambda i: (i, 0)
    tab = lambda i: (i % ntab, 0)
    rows = lambda w: pl.BlockSpec((tm, w), row)
    head_rows = pl.BlockSpec((tm * N_KV_HEADS, HEAD_DIM), row)
    out_shape = [
        jax.ShapeDtypeStruct((m, Q_W), BF16),
        jax.ShapeDtypeStruct((m * N_KV_HEADS, HEAD_DIM), F32),
        jax.ShapeDtypeStruct((m, KV_W), BF16),
        jax.ShapeDtypeStruct((m * N_KV_HEADS, HEAD_DIM), F32),
        jax.ShapeDtypeStruct((m, KV_W), BF16),
        jax.ShapeDtypeStruct((m, D_MODEL), BF16 if fuse_conv else F32),
        jax.ShapeDtypeStruct((m, D_MODEL), BF16),
        jax.ShapeDtypeStruct((m, D_MODEL), BF16),
        jax.ShapeDtypeStruct((nt, n_blk, KV_W), F32),
    ]
    in_specs = [rows(D_MODEL), _const_spec((1, D_MODEL)), _const_spec((D_MODEL, IN_W)),
                _const_spec((1, HEAD_DIM)), _const_spec((1, HEAD_DIM)),
                pl.BlockSpec((tm, HEAD_DIM), tab), pl.BlockSpec((tm, HEAD_DIM), tab),
                pl.BlockSpec((tm, HEAD_DIM), tab)]
    out_specs = [rows(Q_W), head_rows, rows(KV_W), head_rows, rows(KV_W), rows(D_MODEL),
                 rows(D_MODEL), rows(D_MODEL), pl.BlockSpec((1, n_blk, KV_W), lambda i: (i, 0, 0))]
    args = [x2d, g1, w_bf, qg, kg, cos_t, sa_t, sb_t]
    scratch_shapes = []
    tiles_per_seq = None
    if fuse_conv:
        tiles_per_seq = seq // tm
        in_specs += [_const_spec((CONV_WIDTH, D_MODEL))] + [_const_spec((1, D_MODEL))] * 3
        args += list(conv)
        out_shape.append(jax.ShapeDtypeStruct((m // seq, CONV_HALO_ROWS, D_MODEL), F32))
        out_specs.append(pl.BlockSpec((1, CONV_HALO_ROWS, D_MODEL), lambda i: (i // tiles_per_seq, 0, 0)))
        scratch_shapes = [
            pltpu.VMEM((tm + CONV_HALO_ROWS, D_MODEL), F32),
            pltpu.VMEM((SUBLANES - 1, CONV_TILE_ROWS + CONV_HALO_ROWS - SUBLANES, D_MODEL), F32),
            pltpu.VMEM((CONV_TILE_ROWS, D_MODEL), F32)]
    return pl.pallas_call(
        functools.partial(_in_proj_kernel, fuse_conv=fuse_conv, tiles_per_seq=tiles_per_seq),
        out_shape=tuple(out_shape),
        grid=(nt,),
        in_specs=in_specs,
        out_specs=tuple(out_specs),
        scratch_shapes=scratch_shapes,
        compiler_params=pltpu.CompilerParams(dimension_semantics=("arbitrary",),
                                             vmem_limit_bytes=VMEM_LIMIT_BYTES),
        name="in_proj",
    )(*args)


def _split3_bf16(a):
    a1 = a.astype(BF16)
    r1 = a - a1.astype(F32)
    a2 = r1.astype(BF16)
    a3 = (r1 - a2.astype(F32)).astype(BF16)
    return a1, a2, a3


def _block_scores(q_bf, means, blocks_first=False):
    if blocks_first:
        return sum(lax.dot_general(part, q_bf, NT_DIMS, preferred_element_type=F32)
                   for part in _split3_bf16(means))
    return sum(lax.dot_general(q_bf, part, NT_DIMS, preferred_element_type=F32)
               for part in _split3_bf16(means))


def _select_bias(scores, n_elig, block_axis):
    nb = scores.shape[block_axis]
    blk = lax.broadcasted_iota(jnp.int32, scores.shape, block_axis)
    sc = jnp.where(blk < n_elig, scores, NEG_INF)
    rank = jnp.zeros(scores.shape, F32)
    for c in range(nb):
        s_c = sc[c:c + 1, :] if block_axis == 0 else sc[:, c:c + 1]
        tie = jnp.where(blk > c, 1.0, 0.0)
        rank = rank + jnp.where(s_c > sc, 1.0, jnp.where(s_c == sc, tie, 0.0))
    keep = jnp.where(blk < n_elig, jnp.where(rank < MOBA_TOP_K, 1.0, 0.0), 0.0)
    return jnp.where(keep > 0.5, 0.0, NEG_INF)


def _attn_prompt_kernel(q_ref, k_ref, v_ref, ks_ref, o_ref, s_ref):
    i = pl.program_id(2)
    tq = q_ref.shape[0]
    nb = ks_ref.shape[1]
    q2 = jnp.concatenate([q_ref[:, :HEAD_DIM], q_ref[:, HEAD_DIM:]], axis=0)
    rows = q2.shape[0]
    means = ks_ref[0] * (1.0 / MOBA_BLOCK)
    bias_t = _select_bias(_block_scores(q2, means, blocks_first=True), i, 0)
    bias_pad = jnp.concatenate([bias_t, jnp.zeros((HEAD_DIM - nb, rows), F32)], axis=0)
    q_aug = jnp.concatenate([q2, bias_pad.T.astype(BF16)], axis=1)

    def block_start(j):
        return pl.multiple_of(j * MOBA_BLOCK, MOBA_BLOCK)

    def lane_max(mx, s):
        return jnp.maximum(mx, jnp.maximum(s[:, :HEAD_DIM], s[:, HEAD_DIM:]))

    def sweep(n, group_body, carry):
        done = 0
        for u in ATTN_BLOCK_GROUPS:
            count = (n - done) // u
            carry = lax.fori_loop(
                0, count, lambda t, c, done=done, u=u: group_body(done + t * u, u, c), carry)
            done = done + count * u
        return carry

    def past_logits(j0, u, mx):
        kk = k_ref[pl.ds(block_start(j0), u * MOBA_BLOCK), :]
        shape = (u * MOBA_BLOCK, HEAD_DIM)
        blk = j0 + (lax.broadcasted_iota(jnp.int32, shape, 0) >> MOBA_SHIFT)
        lane = lax.broadcasted_iota(jnp.int32, shape, 1)
        pick = jnp.where(lane == blk, 1.0, 0.0).astype(BF16)
        s = lax.dot_general(q_aug, jnp.concatenate([kk, pick], axis=1), NT_DIMS,
                            preferred_element_type=F32)
        for w in range(u):
            sw = s[:, w * MOBA_BLOCK:(w + 1) * MOBA_BLOCK]
            s_ref[j0 + w] = sw
            mx = lane_max(mx, sw)
        return mx

    mx = sweep(i, past_logits, jnp.full((rows, HEAD_DIM), NEG_INF, F32))

    ki = k_ref[pl.ds(block_start(i), MOBA_BLOCK), :]
    s = lax.dot_general(q2, ki, NT_DIMS, preferred_element_type=F32)
    t = lax.broadcasted_iota(jnp.int32, s.shape, 0) % tq
    kpos = lax.broadcasted_iota(jnp.int32, s.shape, 1)
    s = jnp.where(kpos <= t, s, NEG_INF)
    s_ref[i] = s
    mx = lane_max(mx, s)
    m = jnp.broadcast_to(jnp.max(mx, axis=-1, keepdims=True), (rows, HEAD_DIM))

    def weighted_values(j0, u, acc):
        parts = []
        for w in range(u):
            sw = s_ref[j0 + w]
            parts += [jnp.exp2(sw[:, :HEAD_DIM] - m), jnp.exp2(sw[:, HEAD_DIM:] - m)]
        p = jnp.concatenate(parts, axis=1).astype(BF16)
        vv = v_ref[pl.ds(block_start(j0), u * MOBA_BLOCK), :]
        ones = jnp.ones((u * MOBA_BLOCK, HEAD_DIM), BF16)
        return acc + jnp.dot(p, jnp.concatenate([vv, ones], axis=1), preferred_element_type=F32)

    acc = sweep(i + 1, weighted_values, jnp.zeros((rows, 2 * HEAD_DIM), F32))
    out = acc[:, :HEAD_DIM] / acc[:, HEAD_DIM:]
    o_ref[:, :HEAD_DIM] = out[:tq].astype(o_ref.dtype)
    o_ref[:, HEAD_DIM:] = out[tq:].astype(o_ref.dtype)


def _attn_prompt(q_bf, k_bf, v_bf, ksum, batch, seq):
    nb = seq // MOBA_BLOCK
    gw = KV_GROUP * HEAD_DIM
    return pl.pallas_call(
        _attn_prompt_kernel,
        out_shape=jax.ShapeDtypeStruct(q_bf.shape, BF16),
        grid=(batch, N_KV_HEADS, nb),
        in_specs=[pl.BlockSpec((MOBA_BLOCK, gw), lambda b, g, i: (b * nb + i, g)),
                  pl.BlockSpec((seq, HEAD_DIM), lambda b, g, i: (b, g)),
                  pl.BlockSpec((seq, HEAD_DIM), lambda b, g, i: (b, g)),
                  pl.BlockSpec((1, nb, HEAD_DIM), lambda b, g, i: (b, 0, g))],
        out_specs=pl.BlockSpec((MOBA_BLOCK, gw), lambda b, g, i: (b * nb + i, g)),
        scratch_shapes=[pltpu.VMEM((nb, KV_GROUP * MOBA_BLOCK, 2 * HEAD_DIM), F32)],
        compiler_params=pltpu.CompilerParams(
            dimension_semantics=("arbitrary", "arbitrary", "arbitrary"),
            vmem_limit_bytes=VMEM_LIMIT_BYTES),
        name="attn_prompt",
    )(q_bf, k_bf, v_bf, ksum)


def _attn_sample_kernel(pt_ref, q_ref, kn_ref, vn_ref, ck_hbm, cv_hbm, o_ref,
                        ring, sems, vbuf, vsems, logit_ref, sum_ref, p_ref, pc_ref, *, n_seq, n_pages):
    s = pl.program_id(0)
    cp = SAMPLE_CHUNK_PAGES
    n_chunks = n_pages // cp
    chunk_keys = cp * PAGE_SIZE
    blk_per_chunk = chunk_keys // MOBA_BLOCK
    pages_per_blk = MOBA_BLOCK // PAGE_SIZE
    n_blocks = n_pages * PAGE_SIZE // MOBA_BLOCK
    n_q = q_ref.shape[2]
    t_new = kn_ref.shape[2]
    page_rows = PAGE_SIZE * N_KV_HEADS
    max_sel = pc_ref.shape[1]
    assert n_chunks % SAMPLE_RING == 0 and n_blocks <= 32 and max_sel % SAMPLE_V_GROUP == 0

    def page_copy(seq, c, r):
        page = pt_ref[seq, c * cp + r]
        slot = c % SAMPLE_RING
        return pltpu.make_async_copy(ck_hbm.at[page], ring.at[slot, pl.ds(r * page_rows, page_rows)],
                                     sems.at[slot])

    def v_copy(g, blk, slot, r):
        page = pt_ref[s, blk * pages_per_blk + r]
        return pltpu.make_async_copy(cv_hbm.at[page, :, g, :],
                                     vbuf.at[g * max_sel + slot, pl.ds(r * PAGE_SIZE, PAGE_SIZE)],
                                     vsems.at[g])

    def head_rows(slot, g):
        return ring.at[slot][pl.ds(g, chunk_keys, stride=N_KV_HEADS), :]

    def start_chunk(seq, c):
        for r in range(cp):
            page_copy(seq, c, r).start()

    def wait_chunk(seq, c):
        for r in range(cp):
            page_copy(seq, c, r).wait()

    depth = SAMPLE_RING - 1

    @pl.when(s == 0)
    def _():
        for c in range(depth):
            start_chunk(s, c)

        def clear(t, carry):
            vbuf[t] = jnp.zeros(vbuf.shape[1:], F32)
            return carry

        lax.fori_loop(0, vbuf.shape[0], clear, 0)

    qs = [q_ref[0, g] for g in range(N_KV_HEADS)]

    for c in range(n_chunks):
        wait_chunk(s, c)
        nxt = c + depth
        if nxt < n_chunks:
            start_chunk(s, nxt)
        else:
            @pl.when(s + 1 < n_seq)
            def _():
                start_chunk(s + 1, nxt - n_chunks)
        slot = c % SAMPLE_RING
        for g in range(N_KV_HEADS):
            kc = head_rows(slot, g)
            for b in range(blk_per_chunk):
                row = c * blk_per_chunk + b
                sum_ref[row:row + 1, g * HEAD_DIM:(g + 1) * HEAD_DIM] = jnp.sum(
                    kc[b * MOBA_BLOCK:(b + 1) * MOBA_BLOCK], axis=0, keepdims=True)
            logit_ref[g, :, c * chunk_keys:(c + 1) * chunk_keys] = lax.dot_general(
                qs[g], kc.astype(BF16), NT_DIMS, preferred_element_type=F32)

    scores = jnp.concatenate(
        [_block_scores(qs[g], sum_ref[:, g * HEAD_DIM:(g + 1) * HEAD_DIM] * (1.0 / MOBA_BLOCK))
         for g in range(N_KV_HEADS)], axis=0)
    bias_all = _select_bias(scores, n_blocks, 1)
    picked = jnp.where(bias_all == 0.0, 1.0, 0.0)
    e_shape = (n_blocks, n_blocks * MOBA_BLOCK)
    expand = jnp.where(
        (lax.broadcasted_iota(jnp.int32, e_shape, 1) >> MOBA_SHIFT)
        == lax.broadcasted_iota(jnp.int32, e_shape, 0), 1.0, 0.0).astype(BF16)
    blk_lane = lax.broadcasted_iota(jnp.int32, (1, n_blocks), 1)
    half_weight = jnp.left_shift(1, blk_lane & 15).astype(F32)
    low_weight = jnp.where(blk_lane < 16, half_weight, 0.0)
    high_weight = jnp.where(blk_lane >= 16, half_weight, 0.0)

    def to_scalar(v):
        return jnp.sum(v, axis=-1, keepdims=True).astype(jnp.int32)[0, 0]

    masks, denom, own_p = [], [], []
    for g in range(N_KV_HEADS):
        rows_g = slice(g * n_q, (g + 1) * n_q)
        any_row = jnp.max(picked[rows_g], axis=0, keepdims=True)
        masks.append(to_scalar(any_row * low_weight) | (to_scalar(any_row * high_weight) << 16))
        bias = jnp.dot(bias_all[rows_g].astype(BF16), expand, preferred_element_type=F32)
        qf = qs[g].astype(F32)
        kn = kn_ref[0, g]
        trow = lax.broadcasted_iota(jnp.int32, (n_q, 1), 0) % t_new
        own = []
        for tk in range(t_new):
            lo = jnp.sum(qf * kn[tk:tk + 1, :], axis=-1, keepdims=True)
            own.append(jnp.where(trow >= tk, lo, NEG_INF))
        mx = own[0]
        for lo in own[1:]:
            mx = jnp.maximum(mx, lo)
        lm = logit_ref[g] + bias
        mx = jnp.maximum(mx, jnp.max(lm, axis=-1, keepdims=True))
        p = jnp.exp2(lm - mx)
        own_p.append([jnp.exp2(lo - mx) for lo in own])
        den = jnp.sum(p, axis=-1, keepdims=True)
        for po in own_p[g]:
            den = den + po
        denom.append(den)
        for b in range(n_blocks):
            p_ref[g, b] = p[:, b * MOBA_BLOCK:(b + 1) * MOBA_BLOCK]
        pc_ref[g] = jnp.zeros(pc_ref.shape[1:], F32)

    n_sel = []
    for g in range(N_KV_HEADS):
        def issue(blk, count, g=g):
            bit = (masks[g] >> blk) & 1

            @pl.when(bit == 1)
            def _():
                for r in range(pages_per_blk):
                    v_copy(g, blk, count, r).start()
                pc_ref[g, count] = p_ref[g, blk]

            return count + bit

        n_sel.append(lax.fori_loop(0, n_blocks, issue, jnp.int32(0)))

    for g in range(N_KV_HEADS):
        def wait_block(t, carry, g=g):
            for r in range(pages_per_blk):
                v_copy(g, 0, 0, r).wait()
            return carry

        lax.fori_loop(0, n_sel[g], wait_block, 0)
        acc = jnp.zeros((n_q, HEAD_DIM), F32)
        for t in range(max_sel // SAMPLE_V_GROUP):
            first = t * SAMPLE_V_GROUP
            pg = jnp.concatenate([pc_ref[g, first + w] for w in range(SAMPLE_V_GROUP)], axis=1)
            vg = vbuf[g * max_sel + first:g * max_sel + first + SAMPLE_V_GROUP]
            vg = vg.reshape(SAMPLE_V_GROUP * MOBA_BLOCK, HEAD_DIM)
            acc = acc + jnp.dot(pg.astype(BF16), vg.astype(BF16), preferred_element_type=F32)
        vn = vn_ref[0, g]
        for tk in range(t_new):
            acc = acc + own_p[g][tk] * vn[tk:tk + 1, :]
        o_ref[0, g] = acc / denom[g]


def _attn_sample(page_table, q_s, k_new, v_new, cache_k_rows, cache_v):
    n_seq, n_pages = page_table.shape
    n_q, t_new = q_s.shape[2], k_new.shape[2]
    chunk_keys = SAMPLE_CHUNK_PAGES * PAGE_SIZE
    n_blocks = n_pages * PAGE_SIZE // MOBA_BLOCK
    max_sel = min(n_q * MOBA_TOP_K, n_blocks)
    max_sel = -(-max_sel // SAMPLE_V_GROUP) * SAMPLE_V_GROUP
    per_seq = lambda shape: pl.BlockSpec((1,) + shape, lambda s, pt: (s, 0, 0, 0))
    grid_spec = pltpu.PrefetchScalarGridSpec(
        num_scalar_prefetch=1,
        grid=(n_seq,),
        in_specs=[per_seq((N_KV_HEADS, n_q, HEAD_DIM)), per_seq((N_KV_HEADS, t_new, HEAD_DIM)),
                  per_seq((N_KV_HEADS, t_new, HEAD_DIM)),
                  pl.BlockSpec(memory_space=pl.ANY), pl.BlockSpec(memory_space=pl.ANY)],
        out_specs=per_seq((N_KV_HEADS, n_q, HEAD_DIM)),
        scratch_shapes=[pltpu.VMEM((SAMPLE_RING, chunk_keys * N_KV_HEADS, HEAD_DIM), F32),
                        pltpu.SemaphoreType.DMA((SAMPLE_RING,)),
                        pltpu.VMEM((N_KV_HEADS * max_sel, MOBA_BLOCK, HEAD_DIM), F32),
                        pltpu.SemaphoreType.DMA((N_KV_HEADS,)),
                        pltpu.VMEM((N_KV_HEADS, n_q, n_pages * PAGE_SIZE), F32),
                        pltpu.VMEM((n_blocks, KV_W), F32),
                        pltpu.VMEM((N_KV_HEADS, n_blocks, n_q, MOBA_BLOCK), F32),
                        pltpu.VMEM((N_KV_HEADS, max_sel, n_q, MOBA_BLOCK), F32)],
    )
    return pl.pallas_call(
        functools.partial(_attn_sample_kernel, n_seq=n_seq, n_pages=n_pages),
        out_shape=jax.ShapeDtypeStruct(q_s.shape, F32),
        grid_spec=grid_spec,
        compiler_params=pltpu.CompilerParams(dimension_semantics=("arbitrary",),
                                             vmem_limit_bytes=VMEM_LIMIT_BYTES),
        name="attn_sample",
    )(page_table, q_s, k_new, v_new, cache_k_rows, cache_v)


def _conv_sample_kernel(st_ref, u0_ref, u1_ref, u2_ref, u3_ref, wst_ref, wnew_ref, cb_ref, lg_ref,
                        lb_ref, o0_ref, o1_ref, o2_ref, o3_ref):
    st = st_ref[...]
    new = [u0_ref[...], u1_ref[...], u2_ref[...], u3_ref[...]]
    outs = [o0_ref, o1_ref, o2_ref, o3_ref]
    for t in range(len(new)):
        yc = jnp.sum(st * wst_ref[t][None, :, :], axis=1) + cb_ref[...]
        for tk in range(t + 1):
            yc = yc + new[tk] * wnew_ref[t, tk:tk + 1, :]
        outs[t][...] = _ln_swish(yc, lg_ref[...], lb_ref[...]).astype(outs[t].dtype)


def _conv_sample(state, u_new, conv_w, conv_b, ln_g, ln_b):
    n_seq = state.shape[0]
    t_new = len(u_new)
    ts = 16
    w_state = jnp.stack([jnp.concatenate([jnp.zeros((t, D_MODEL), F32), conv_w[:CONV_STATE - t]], axis=0)
                         for t in range(t_new)])
    w_new = jnp.stack([jnp.concatenate([conv_w[CONV_STATE - t:], jnp.zeros((t_new - 1 - t, D_MODEL), F32)],
                                       axis=0) for t in range(t_new)])
    seqs = pl.BlockSpec((ts, D_MODEL), lambda i: (i, 0))
    return pl.pallas_call(
        _conv_sample_kernel,
        out_shape=tuple(jax.ShapeDtypeStruct((n_seq, D_MODEL), BF16) for _ in range(t_new)),
        grid=(n_seq // ts,),
        in_specs=[pl.BlockSpec((ts, CONV_STATE, D_MODEL), lambda i: (i, 0, 0))] + [seqs] * t_new + [
            _const_spec((t_new, CONV_STATE, D_MODEL)), _const_spec((t_new, t_new, D_MODEL)),
            _const_spec((1, D_MODEL)), _const_spec((1, D_MODEL)), _const_spec((1, D_MODEL))],
        out_specs=tuple(seqs for _ in range(t_new)),
        compiler_params=pltpu.CompilerParams(dimension_semantics=("arbitrary",),
                                             vmem_limit_bytes=VMEM_LIMIT_BYTES),
        name="conv_sample",
    )(state, *u_new, w_state, w_new, conv_b, ln_g, ln_b)


def _post_kernel(c_ref, sgc_ref, sga_ref, o_ref, x_ref, wco_ref, wo_ref, g2_ref, wfi_ref, wfo_ref,
                 y_ref):
    yc = jnp.dot(c_ref[...], wco_ref[...], preferred_element_type=F32)
    m = sgc_ref[...].astype(F32) * yc + sga_ref[...].astype(F32) * o_ref[...].astype(F32)
    x1 = x_ref[...] + jnp.dot(m.astype(BF16), wo_ref[...], preferred_element_type=F32)
    h2 = (x1 * lax.rsqrt(jnp.mean(x1 * x1, axis=-1, keepdims=True) + NORM_EPS) * g2_ref[...]).astype(BF16)
    ab = jnp.dot(h2, wfi_ref[...], preferred_element_type=F32)
    a, b = ab[:, :FFN_HIDDEN], ab[:, FFN_HIDDEN:]
    hid = (a * _sigmoid(a) * b).astype(BF16)
    y_ref[...] = x1 + jnp.dot(hid, wfo_ref[...], preferred_element_type=F32)


def _post(c_act, sgc, sga, o_attn, x2d, wco_bf, wo_bf, g2, wfi_bf, wfo_bf):
    m = x2d.shape[0]
    tm = POST_TILE_ROWS
    rows = pl.BlockSpec((tm, D_MODEL), lambda i: (i, 0))
    return pl.pallas_call(
        _post_kernel,
        out_shape=jax.ShapeDtypeStruct((m, D_MODEL), F32),
        grid=(m // tm,),
        in_specs=[rows, rows, rows, rows, rows,
                  _const_spec((D_MODEL, D_MODEL)), _const_spec((D_MODEL, D_MODEL)),
                  _const_spec((1, D_MODEL)), _const_spec((D_MODEL, 2 * FFN_HIDDEN)),
                  _const_spec((FFN_HIDDEN, D_MODEL))],
        out_specs=rows,
        compiler_params=pltpu.CompilerParams(dimension_semantics=("arbitrary",),
                                             vmem_limit_bytes=VMEM_LIMIT_BYTES),
        name="post",
    )(c_act, sgc, sga, o_attn, x2d, wco_bf, wo_bf, g2, wfi_bf, wfo_bf)


def _rope_tables(pos):
    half = ROT_DIM // 2
    inv_freq = ROPE_THETA ** (-(jnp.arange(half, dtype=F32) * 2.0 / ROT_DIM))
    ang = pos.astype(F32)[:, None] * inv_freq[None, :]
    cos, sin = jnp.cos(ang), jnp.sin(ang)
    n = pos.shape[0]
    zeros = lambda w: jnp.zeros((n, w), F32)
    cos_t = jnp.concatenate([cos, cos, jnp.ones((n, HEAD_DIM - ROT_DIM), F32)], axis=1)
    sa_t = jnp.concatenate([-sin, zeros(HEAD_DIM - half)], axis=1)
    sb_t = jnp.concatenate([zeros(half), sin, zeros(HEAD_DIM - ROT_DIM)], axis=1)
    return cos_t, sa_t, sb_t


def kernel(x_prompt, x_sample, cache_k, cache_v, state_conv, page_table, norm1_g, w_in, q_norm_g,
           k_norm_g, conv_dw_w, conv_dw_b, conv_ln_g, conv_ln_b, w_conv_out, w_out, norm2_g,
           w_ffn_in, w_ffn_out):
    batch, seq, _ = x_prompt.shape
    n_seq, t_new, _ = x_sample.shape
    n_phys = cache_k.shape[0]
    past_len = page_table.shape[1] * PAGE_SIZE
    assert seq % IN_TILE_ROWS == 0 and (n_seq * t_new) % IN_TILE_ROWS == 0
    assert IN_TILE_ROWS % t_new == 0 and past_len % MOBA_BLOCK == 0 and seq >= CONV_STATE

    row = lambda v: v.reshape(1, -1).astype(F32)
    w_in_bf, wco_bf, wo_bf = w_in.astype(BF16), w_conv_out.astype(BF16), w_out.astype(BF16)
    wfi_bf, wfo_bf = w_ffn_in.astype(BF16), w_ffn_out.astype(BF16)
    g1, g2, qg, kg = row(norm1_g), row(norm2_g), row(q_norm_g), row(k_norm_g)
    conv_b, ln_g, ln_b = row(conv_dw_b), row(conv_ln_g), row(conv_ln_b)

    xp = x_prompt.reshape(batch * seq, D_MODEL)
    q_p, k_p, kbf_p, v_p, vbf_p, c_p, sgc_p, sga_p, ksum_p, u_tail_p = _in_proj(
        xp, g1, w_in_bf, qg, kg, *_rope_tables(jnp.arange(seq)),
        conv=(conv_dw_w, conv_b, ln_g, ln_b), seq=seq)
    nb = seq // MOBA_BLOCK
    o_p = _attn_prompt(q_p, kbf_p, vbf_p, ksum_p.reshape(batch, nb, KV_W), batch, seq)
    y_p = _post(c_p, sgc_p, sga_p, o_p, xp, wco_bf, wo_bf, g2, wfi_bf, wfo_bf)

    xs = x_sample.reshape(n_seq * t_new, D_MODEL)
    pos_s = past_len + jnp.arange(IN_TILE_ROWS) % t_new
    q_s, k_s, _, v_s, _, u_s, sgc_s, sga_s, _ = _in_proj(
        xs, g1, w_in_bf, qg, kg, *_rope_tables(pos_s))
    q_g = q_s.reshape(n_seq, t_new, N_KV_HEADS, KV_GROUP, HEAD_DIM).transpose(0, 2, 3, 1, 4)
    q_g = q_g.reshape(n_seq, N_KV_HEADS, KV_GROUP * t_new, HEAD_DIM)
    k_new = k_s.reshape(n_seq, t_new, N_KV_HEADS, HEAD_DIM).transpose(0, 2, 1, 3)
    v_new = v_s.reshape(n_seq, t_new, N_KV_HEADS, HEAD_DIM).transpose(0, 2, 1, 3)
    page_rows = PAGE_SIZE * N_KV_HEADS
    o_g = _attn_sample(page_table, q_g, k_new, v_new, cache_k.reshape(n_phys, page_rows, HEAD_DIM),
                       cache_v)
    o_s = o_g.reshape(n_seq, N_KV_HEADS, KV_GROUP, t_new, HEAD_DIM).transpose(0, 3, 1, 2, 4)
    o_s = o_s.reshape(n_seq * t_new, D_MODEL).astype(BF16)
    u_s3 = u_s.reshape(n_seq, t_new, D_MODEL)
    c_steps = _conv_sample(state_conv, [u_s3[:, t] for t in range(t_new)], conv_dw_w, conv_b, ln_g, ln_b)
    c_s = jnp.stack(c_steps, axis=1).reshape(n_seq * t_new, D_MODEL)
    y_s = _post(c_s, sgc_s, sga_s, o_s, xs, wco_bf, wo_bf, g2, wfi_bf, wfo_bf)

    kv_shape_p = (batch, seq, N_KV_HEADS, HEAD_DIM)
    kv_shape_s = (n_seq, t_new, N_KV_HEADS, HEAD_DIM)
    conv_prompt_state = u_tail_p[:, CONV_HALO_ROWS - CONV_STATE:]
    conv_sample_state = jnp.concatenate([state_conv, u_s3], axis=1)[:, t_new:]
    return (y_p.reshape(batch, seq, D_MODEL), y_s.reshape(n_seq, t_new, D_MODEL),
            k_p.reshape(kv_shape_p), v_p.reshape(kv_shape_p), conv_prompt_state,
            k_s.reshape(kv_shape_s), v_s.reshape(kv_shape_s), conv_sample_state)
```

```python
import functools

import jax
import jax.numpy as jnp
from jax import lax
from jax.experimental import pallas as pl
from jax.experimental.pallas import tpu as pltpu

F32 = jnp.float32
BF16 = jnp.bfloat16

D_MODEL = 1024
N_HEADS = 8
N_KV_HEADS = 4
HEAD_DIM = 128
KV_GROUP = N_HEADS // N_KV_HEADS
ROT_DIM = HEAD_DIM // 4
ROPE_THETA = 500000.0
MOBA_BLOCK = 256
MOBA_SHIFT = MOBA_BLOCK.bit_length() - 1
ATTN_BLOCK_GROUPS = (8, 4, 2, 1)
MOBA_TOP_K = 3
PAGE_SIZE = 128
CONV_WIDTH = 31
CONV_STATE = CONV_WIDTH - 1
FFN_HIDDEN = 2816
NORM_EPS = 1e-6
NEG_INF = -1e30
Q_W = N_HEADS * HEAD_DIM
KV_W = N_KV_HEADS * HEAD_DIM
IN_W = Q_W + 2 * KV_W + 4 * D_MODEL
ATTN_SCALE = HEAD_DIM ** -0.5
LOG2_E = 1.4426950408889634
Q_PRESCALE = ATTN_SCALE * LOG2_E

VMEM_LIMIT_BYTES = 56 * 1024 * 1024

IN_TILE_ROWS = 512
POST_TILE_ROWS = 256
CONV_TILE_ROWS = 128
CONV_HALO_ROWS = 32
CONV_ROW_CHUNK = 64
SUBLANES = 8
SAMPLE_CHUNK_PAGES = 8
SAMPLE_RING = 4
SAMPLE_V_GROUP = 4

NT_DIMS = (((1,), (1,)), ((), ()))


def _sigmoid(x):
    return 1.0 / (1.0 + jnp.exp(-x))


def _const_spec(shape):
    zeros = (0,) * len(shape)
    return pl.BlockSpec(shape, lambda *_: zeros, pipeline_mode=pl.Buffered(1))


def _ln_swish(yc, g, b):
    xc = yc - jnp.mean(yc, axis=-1, keepdims=True)
    var = jnp.mean(xc * xc, axis=-1, keepdims=True)
    z = xc * lax.rsqrt(var + NORM_EPS) * g + b
    return z * _sigmoid(z)


def _causal_conv_tile(ext_ref, sh_ref, yc_ref, base, w_ref, cb_ref):
    lead = CONV_HALO_ROWS - CONV_STATE
    sh_rows = sh_ref.shape[1]
    for ph in range(1, SUBLANES):
        sh_ref[ph - 1] = ext_ref[base + ph:base + ph + sh_rows, :]
    for rc in range(CONV_TILE_ROWS // CONV_ROW_CHUNK):
        for cc in range(D_MODEL // HEAD_DIM):
            lanes = slice(cc * HEAD_DIM, (cc + 1) * HEAD_DIM)
            acc = jnp.broadcast_to(cb_ref[:, lanes], (CONV_ROW_CHUNK, HEAD_DIM))
            for j in range(CONV_WIDTH):
                ph = (lead + j) % SUBLANES
                r0 = rc * CONV_ROW_CHUNK + (lead + j) - ph
                rows = (ext_ref[base + r0:base + r0 + CONV_ROW_CHUNK, lanes] if ph == 0
                        else sh_ref[ph - 1, r0:r0 + CONV_ROW_CHUNK, lanes])
                acc = acc + w_ref[j:j + 1, lanes] * rows
            yc_ref[rc * CONV_ROW_CHUNK:(rc + 1) * CONV_ROW_CHUNK, lanes] = acc


def _in_proj_kernel(*refs, fuse_conv, tiles_per_seq):
    x_ref, g1_ref, w_ref, qg_ref, kg_ref, cos_ref, sa_ref, sb_ref = refs[:8]
    if fuse_conv:
        cw_ref, cb_ref, lg_ref, lb_ref = refs[8:12]
        (q_ref, k32_ref, kbf_ref, v32_ref, vbf_ref, c_ref, sgc_ref, sga_ref, ksum_ref, tail_out_ref,
         ext_ref, sh_ref, yc_ref) = refs[12:]
    else:
        q_ref, k32_ref, kbf_ref, v32_ref, vbf_ref, u_ref, sgc_ref, sga_ref, ksum_ref = refs[8:]
    tm = x_ref.shape[0]
    x = x_ref[...]
    h = (x * lax.rsqrt(jnp.mean(x * x, axis=-1, keepdims=True) + NORM_EPS) * g1_ref[...]).astype(BF16)
    cos, sa, sb = cos_ref[...], sa_ref[...], sb_ref[...]
    half = ROT_DIM // 2

    def norm_rope(z, g):
        zn = z * lax.rsqrt(jnp.mean(z * z, axis=-1, keepdims=True) + NORM_EPS) * g
        return zn * cos + pltpu.roll(zn, HEAD_DIM - half, 1) * sa + pltpu.roll(zn, half, 1) * sb

    def proj(c0, width):
        return jnp.dot(h, w_ref[:, c0:c0 + width], preferred_element_type=F32)

    c0 = Q_W + 2 * KV_W
    u = proj(c0, D_MODEL) * _sigmoid(proj(c0 + D_MODEL, D_MODEL))
    if fuse_conv:
        first = pl.program_id(0) % tiles_per_seq == 0

        @pl.when(first)
        def _():
            ext_ref[0:CONV_HALO_ROWS, :] = jnp.zeros((CONV_HALO_ROWS, D_MODEL), F32)

        @pl.when(jnp.logical_not(first))
        def _():
            ext_ref[0:CONV_HALO_ROWS, :] = ext_ref[tm:tm + CONV_HALO_ROWS, :]

        ext_ref[CONV_HALO_ROWS:, :] = u
        tail_out_ref[0] = u[tm - CONV_HALO_ROWS:]
        for sub in range(tm // CONV_TILE_ROWS):
            _causal_conv_tile(ext_ref, sh_ref, yc_ref, sub * CONV_TILE_ROWS, cw_ref, cb_ref)
            c_ref[sub * CONV_TILE_ROWS:(sub + 1) * CONV_TILE_ROWS, :] = _ln_swish(
                yc_ref[...], lg_ref[...], lb_ref[...]).astype(c_ref.dtype)
    else:
        u_ref[...] = u

    zq = proj(0, Q_W)
    qg = qg_ref[...]
    for hh in range(N_HEADS):
        sl = slice(hh * HEAD_DIM, (hh + 1) * HEAD_DIM)
        q_ref[:, sl] = (norm_rope(zq[:, sl], qg) * Q_PRESCALE).astype(BF16)

    zkv = proj(Q_W, 2 * KV_W)
    kg = kg_ref[...]
    n_blk = ksum_ref.shape[1]
    for g in range(N_KV_HEADS):
        sl = slice(g * HEAD_DIM, (g + 1) * HEAD_DIM)
        kk = norm_rope(zkv[:, sl], kg)
        vv = zkv[:, KV_W + g * HEAD_DIM:KV_W + (g + 1) * HEAD_DIM]
        k32_ref[pl.ds(g, tm, stride=N_KV_HEADS), :] = kk
        v32_ref[pl.ds(g, tm, stride=N_KV_HEADS), :] = vv
        kbf_ref[:, sl] = kk.astype(BF16)
        vbf_ref[:, sl] = vv.astype(BF16)
        for blk in range(n_blk):
            ksum_ref[0, blk:blk + 1, sl] = jnp.sum(
                kk[blk * MOBA_BLOCK:(blk + 1) * MOBA_BLOCK], axis=0, keepdims=True)

    sgc_ref[...] = _sigmoid(proj(c0 + 2 * D_MODEL, D_MODEL)).astype(BF16)
    sga_ref[...] = _sigmoid(proj(c0 + 3 * D_MODEL, D_MODEL)).astype(BF16)


def _in_proj(x2d, g1, w_bf, qg, kg, cos_t, sa_t, sb_t, conv=None, seq=None):
    m = x2d.shape[0]
    tm = IN_TILE_ROWS
    nt = m // tm
    ntab = cos_t.shape[0] // tm
    n_blk = tm // MOBA_BLOCK
    fuse_conv = conv is not None
    row = lambda i: (i, 0)
    tab = lambda i: (i % ntab, 0)
    rows = lambda w: pl.BlockSpec((tm, w), row)
    head_rows = pl.BlockSpec((tm * N_KV_HEADS, HEAD_DIM), row)
    out_shape = [
        jax.ShapeDtypeStruct((m, Q_W), BF16),
        jax.ShapeDtypeStruct((m * N_KV_HEADS, HEAD_DIM), F32),
        jax.ShapeDtypeStruct((m, KV_W), BF16),
        jax.ShapeDtypeStruct((m * N_KV_HEADS, HEAD_DIM), F32),
        jax.ShapeDtypeStruct((m, KV_W), BF16),
        jax.ShapeDtypeStruct((m, D_MODEL), BF16 if fuse_conv else F32),
        jax.ShapeDtypeStruct((m, D_MODEL), BF16),
        jax.ShapeDtypeStruct((m, D_MODEL), BF16),
        jax.ShapeDtypeStruct((nt, n_blk, KV_W), F32),
    ]
    in_specs = [rows(D_MODEL), _const_spec((1, D_MODEL)), _const_spec((D_MODEL, IN_W)),
                _const_spec((1, HEAD_DIM)), _const_spec((1, HEAD_DIM)),
                pl.BlockSpec((tm, HEAD_DIM), tab), pl.BlockSpec((tm, HEAD_DIM), tab),
                pl.BlockSpec((tm, HEAD_DIM), tab)]
    out_specs = [rows(Q_W), head_rows, rows(KV_W), head_rows, rows(KV_W), rows(D_MODEL),
                 rows(D_MODEL), rows(D_MODEL), pl.BlockSpec((1, n_blk, KV_W), lambda i: (i, 0, 0))]
    args = [x2d, g1, w_bf, qg, kg, cos_t, sa_t, sb_t]
    scratch_shapes = []
    tiles_per_seq = None
    if fuse_conv:
        tiles_per_seq = seq // tm
        in_specs += [_const_spec((CONV_WIDTH, D_MODEL))] + [_const_spec((1, D_MODEL))] * 3
        args += list(conv)
        out_shape.append(jax.ShapeDtypeStruct((m // seq, CONV_HALO_ROWS, D_MODEL), F32))
        out_specs.append(pl.BlockSpec((1, CONV_HALO_ROWS, D_MODEL), lambda i: (i // tiles_per_seq, 0, 0)))
        scratch_shapes = [
            pltpu.VMEM((tm + CONV_HALO_ROWS, D_MODEL), F32),
            pltpu.VMEM((SUBLANES - 1, CONV_TILE_ROWS + CONV_HALO_ROWS - SUBLANES, D_MODEL), F32),
            pltpu.VMEM((CONV_TILE_ROWS, D_MODEL), F32)]
    return pl.pallas_call(
        functools.partial(_in_proj_kernel, fuse_conv=fuse_conv, tiles_per_seq=tiles_per_seq),
        out_shape=tuple(out_shape),
        grid=(nt,),
        in_specs=in_specs,
        out_specs=tuple(out_specs),
        scratch_shapes=scratch_shapes,
        compiler_params=pltpu.CompilerParams(dimension_semantics=("arbitrary",),
                                             vmem_limit_bytes=VMEM_LIMIT_BYTES),
        name="in_proj",
    )(*args)


def _split3_bf16(a):
    a1 = a.astype(BF16)
    r1 = a - a1.astype(F32)
    a2 = r1.astype(BF16)
    a3 = (r1 - a2.astype(F32)).astype(BF16)
    return a1, a2, a3


def _block_scores(q_bf, means, blocks_first=False):
    if blocks_first:
        return sum(lax.dot_general(part, q_bf, NT_DIMS, preferred_element_type=F32)
                   for part in _split3_bf16(means))
    return sum(lax.dot_general(q_bf, part, NT_DIMS, preferred_element_type=F32)
               for part in _split3_bf16(means))


def _select_bias(scores, n_elig, block_axis):
    nb = scores.shape[block_axis]
    blk = lax.broadcasted_iota(jnp.int32, scores.shape, block_axis)
    sc = jnp.where(blk < n_elig, scores, NEG_INF)
    rank = jnp.zeros(scores.shape, F32)
    for c in range(nb):
        s_c = sc[c:c + 1, :] if block_axis == 0 else sc[:, c:c + 1]
        tie = jnp.where(blk > c, 1.0, 0.0)
        rank = rank + jnp.where(s_c > sc, 1.0, jnp.where(s_c == sc, tie, 0.0))
    keep = jnp.where(blk < n_elig, jnp.where(rank < MOBA_TOP_K, 1.0, 0.0), 0.0)
    return jnp.where(keep > 0.5, 0.0, NEG_INF)


def _attn_prompt_kernel(q_ref, k_ref, v_ref, ks_ref, o_ref, s_ref):
    i = pl.program_id(2)
    tq = q_ref.shape[0]
    nb = ks_ref.shape[1]
    q2 = jnp.concatenate([q_ref[:, :HEAD_DIM], q_ref[:, HEAD_DIM:]], axis=0)
    rows = q2.shape[0]
    means = ks_ref[0] * (1.0 / MOBA_BLOCK)
    bias_t = _select_bias(_block_scores(q2, means, blocks_first=True), i, 0)
    bias_pad = jnp.concatenate([bias_t, jnp.zeros((HEAD_DIM - nb, rows), F32)], axis=0)
    q_aug = jnp.concatenate([q2, bias_pad.T.astype(BF16)], axis=1)

    def block_start(j):
        return pl.multiple_of(j * MOBA_BLOCK, MOBA_BLOCK)

    def lane_max(mx, s):
        return jnp.maximum(mx, jnp.maximum(s[:, :HEAD_DIM], s[:, HEAD_DIM:]))

    def sweep(n, group_body, carry):
        done = 0
        for u in ATTN_BLOCK_GROUPS:
            count = (n - done) // u
            carry = lax.fori_loop(
                0, count, lambda t, c, done=done, u=u: group_body(done + t * u, u, c), carry)
            done = done + count * u
        return carry

    def past_logits(j0, u, mx):
        kk = k_ref[pl.ds(block_start(j0), u * MOBA_BLOCK), :]
        shape = (u * MOBA_BLOCK, HEAD_DIM)
        blk = j0 + (lax.broadcasted_iota(jnp.int32, shape, 0) >> MOBA_SHIFT)
        lane = lax.broadcasted_iota(jnp.int32, shape, 1)
        pick = jnp.where(lane == blk, 1.0, 0.0).astype(BF16)
        s = lax.dot_general(q_aug, jnp.concatenate([kk, pick], axis=1), NT_DIMS,
                            preferred_element_type=F32)
        for w in range(u):
            sw = s[:, w * MOBA_BLOCK:(w + 1) * MOBA_BLOCK]
            s_ref[j0 + w] = sw
            mx = lane_max(mx, sw)
        return mx

    ki = k_ref[pl.ds(block_start(i), MOBA_BLOCK), :]
    s = lax.dot_general(q2, ki, NT_DIMS, preferred_element_type=F32)
    t = lax.broadcasted_iota(jnp.int32, s.shape, 0) % tq
    kpos = lax.broadcasted_iota(jnp.int32, s.shape, 1)
    s = jnp.where(kpos <= t, s, NEG_INF)
    s_ref[i] = s
    mx = lane_max(jnp.full((rows, HEAD_DIM), NEG_INF, F32), s)

    mx = sweep(i, past_logits, mx)
    m = jnp.broadcast_to(jnp.max(mx, axis=-1, keepdims=True), (rows, HEAD_DIM))

    def weighted_values(j0, u, acc):
        parts = []
        for w in range(u):
            sw = s_ref[j0 + w]
            parts += [jnp.exp2(sw[:, :HEAD_DIM] - m), jnp.exp2(sw[:, HEAD_DIM:] - m)]
        p = jnp.concatenate(parts, axis=1).astype(BF16)
        vv = v_ref[pl.ds(block_start(j0), u * MOBA_BLOCK), :]
        ones = jnp.ones((u * MOBA_BLOCK, HEAD_DIM), BF16)
        return acc + jnp.dot(p, jnp.concatenate([vv, ones], axis=1), preferred_element_type=F32)

    acc = sweep(i + 1, weighted_values, jnp.zeros((rows, 2 * HEAD_DIM), F32))
    out = acc[:, :HEAD_DIM] / acc[:, HEAD_DIM:]
    o_ref[:, :HEAD_DIM] = out[:tq].astype(o_ref.dtype)
    o_ref[:, HEAD_DIM:] = out[tq:].astype(o_ref.dtype)


def _attn_prompt(q_bf, k_bf, v_bf, ksum, batch, seq):
    nb = seq // MOBA_BLOCK
    gw = KV_GROUP * HEAD_DIM
    return pl.pallas_call(
        _attn_prompt_kernel,
        out_shape=jax.ShapeDtypeStruct(q_bf.shape, BF16),
        grid=(batch, N_KV_HEADS, nb),
        in_specs=[pl.BlockSpec((MOBA_BLOCK, gw), lambda b, g, i: (b * nb + i, g)),
                  pl.BlockSpec((seq, HEAD_DIM), lambda b, g, i: (b, g)),
                  pl.BlockSpec((seq, HEAD_DIM), lambda b, g, i: (b, g)),
                  pl.BlockSpec((1, nb, HEAD_DIM), lambda b, g, i: (b, 0, g))],
        out_specs=pl.BlockSpec((MOBA_BLOCK, gw), lambda b, g, i: (b * nb + i, g)),
        scratch_shapes=[pltpu.VMEM((nb, KV_GROUP * MOBA_BLOCK, 2 * HEAD_DIM), F32)],
        compiler_params=pltpu.CompilerParams(
            dimension_semantics=("arbitrary", "arbitrary", "arbitrary"),
            vmem_limit_bytes=VMEM_LIMIT_BYTES),
        name="attn_prompt",
    )(q_bf, k_bf, v_bf, ksum)


def _attn_sample_kernel(pt_ref, q_ref, kn_ref, vn_ref, ck_hbm, cv_hbm, o_ref,
                        ring, sems, vbuf, vsems, logit_ref, sum_ref, p_ref, pc_ref, own_ref, den_ref,
                        nsel_ref, *, n_seq, n_pages):
    s = pl.program_id(0)
    cp = SAMPLE_CHUNK_PAGES
    n_chunks = n_pages // cp
    chunk_keys = cp * PAGE_SIZE
    blk_per_chunk = chunk_keys // MOBA_BLOCK
    pages_per_blk = MOBA_BLOCK // PAGE_SIZE
    n_blocks = n_pages * PAGE_SIZE // MOBA_BLOCK
    n_q = q_ref.shape[2]
    t_new = kn_ref.shape[2]
    page_rows = PAGE_SIZE * N_KV_HEADS
    max_sel = pc_ref.shape[1]
    assert n_chunks % SAMPLE_RING == 0 and n_blocks <= 32 and max_sel % SAMPLE_V_GROUP == 0

    def page_copy(seq, c, r):
        page = pt_ref[seq, c * cp + r]
        slot = c % SAMPLE_RING
        return pltpu.make_async_copy(ck_hbm.at[page], ring.at[slot, pl.ds(r * page_rows, page_rows)],
                                     sems.at[slot])

    def v_copy(g, blk, slot, r):
        page = pt_ref[jnp.minimum(s, n_seq - 1), blk * pages_per_blk + r]
        return pltpu.make_async_copy(cv_hbm.at[page, :, g, :],
                                     vbuf.at[g * max_sel + slot, pl.ds(r * PAGE_SIZE, PAGE_SIZE)],
                                     vsems.at[g])

    def head_rows(slot, g):
        return ring.at[slot][pl.ds(g, chunk_keys, stride=N_KV_HEADS), :]

    def start_chunk(seq, c):
        for r in range(cp):
            page_copy(seq, c, r).start()

    def wait_chunk(seq, c):
        for r in range(cp):
            page_copy(seq, c, r).wait()

    depth = SAMPLE_RING - 1

    @pl.when(s == 0)
    def _():
        for c in range(depth):
            start_chunk(s, c)

        def clear(t, carry):
            vbuf[t] = jnp.zeros(vbuf.shape[1:], F32)
            return carry

        lax.fori_loop(0, vbuf.shape[0], clear, 0)

    def pick_and_fetch():
        qs = [q_ref[0, g] for g in range(N_KV_HEADS)]
        scores = jnp.concatenate(
            [_block_scores(qs[g], sum_ref[:, g * HEAD_DIM:(g + 1) * HEAD_DIM] * (1.0 / MOBA_BLOCK))
             for g in range(N_KV_HEADS)], axis=0)
        bias_all = _select_bias(scores, n_blocks, 1)
        picked = jnp.where(bias_all == 0.0, 1.0, 0.0)
        e_shape = (n_blocks, n_blocks * MOBA_BLOCK)
        expand = jnp.where(
            (lax.broadcasted_iota(jnp.int32, e_shape, 1) >> MOBA_SHIFT)
            == lax.broadcasted_iota(jnp.int32, e_shape, 0), 1.0, 0.0).astype(BF16)
        blk_lane = lax.broadcasted_iota(jnp.int32, (1, n_blocks), 1)
        half_weight = jnp.left_shift(1, blk_lane & 15).astype(F32)
        low_weight = jnp.where(blk_lane < 16, half_weight, 0.0)
        high_weight = jnp.where(blk_lane >= 16, half_weight, 0.0)

        def to_scalar(v):
            return jnp.sum(v, axis=-1, keepdims=True).astype(jnp.int32)[0, 0]

        masks = []
        for g in range(N_KV_HEADS):
            rows_g = slice(g * n_q, (g + 1) * n_q)
            any_row = jnp.max(picked[rows_g], axis=0, keepdims=True)
            masks.append(to_scalar(any_row * low_weight) | (to_scalar(any_row * high_weight) << 16))
            bias = jnp.dot(bias_all[rows_g].astype(BF16), expand, preferred_element_type=F32)
            qf = qs[g].astype(F32)
            kn = kn_ref[0, g]
            trow = lax.broadcasted_iota(jnp.int32, (n_q, 1), 0) % t_new
            own = []
            for tk in range(t_new):
                lo = jnp.sum(qf * kn[tk:tk + 1, :], axis=-1, keepdims=True)
                own.append(jnp.where(trow >= tk, lo, NEG_INF))
            mx = own[0]
            for lo in own[1:]:
                mx = jnp.maximum(mx, lo)
            lm = logit_ref[g] + bias
            mx = jnp.maximum(mx, jnp.max(lm, axis=-1, keepdims=True))
            p = jnp.exp2(lm - mx)
            den = jnp.sum(p, axis=-1, keepdims=True)
            vn = vn_ref[0, g]
            own_out = jnp.zeros((n_q, HEAD_DIM), F32)
            for tk in range(t_new):
                p_own = jnp.exp2(own[tk] - mx)
                den = den + p_own
                own_out = own_out + p_own * vn[tk:tk + 1, :]
            own_ref[g] = own_out
            den_ref[g] = jnp.broadcast_to(den, (n_q, HEAD_DIM))
            for b in range(n_blocks):
                p_ref[g, b] = p[:, b * MOBA_BLOCK:(b + 1) * MOBA_BLOCK]
            pc_ref[g] = jnp.zeros(pc_ref.shape[1:], F32)

        for g in range(N_KV_HEADS):
            def issue(blk, count, g=g):
                bit = (masks[g] >> blk) & 1

                @pl.when(bit == 1)
                def _():
                    for r in range(pages_per_blk):
                        v_copy(g, blk, count, r).start()
                    pc_ref[g, count] = p_ref[g, blk]

                return count + bit

            nsel_ref[g] = lax.fori_loop(0, n_blocks, issue, jnp.int32(0))

    @pl.when(s < n_seq)
    def _():
        qs = [q_ref[0, g] for g in range(N_KV_HEADS)]
        for c in range(n_chunks):
            wait_chunk(s, c)
            nxt = c + depth
            if nxt < n_chunks:
                start_chunk(s, nxt)
            else:
                @pl.when(s + 1 < n_seq)
                def _():
                    start_chunk(s + 1, nxt - n_chunks)
            slot = c % SAMPLE_RING
            for g in range(N_KV_HEADS):
                kc = head_rows(slot, g)
                for b in range(blk_per_chunk):
                    row = c * blk_per_chunk + b
                    sum_ref[row:row + 1, g * HEAD_DIM:(g + 1) * HEAD_DIM] = jnp.sum(
                        kc[b * MOBA_BLOCK:(b + 1) * MOBA_BLOCK], axis=0, keepdims=True)
                logit_ref[g, :, c * chunk_keys:(c + 1) * chunk_keys] = lax.dot_general(
                    qs[g], kc.astype(BF16), NT_DIMS, preferred_element_type=F32)

    @pl.when(s > 0)
    def _():
        for g in range(N_KV_HEADS):
            def wait_block(t, carry, g=g):
                for r in range(pages_per_blk):
                    v_copy(g, 0, 0, r).wait()
                return carry

            lax.fori_loop(0, nsel_ref[g], wait_block, 0)
            acc = own_ref[g]
            for t in range(max_sel // SAMPLE_V_GROUP):
                first = t * SAMPLE_V_GROUP
                pg = jnp.concatenate([pc_ref[g, first + w] for w in range(SAMPLE_V_GROUP)], axis=1)
                vg = vbuf[g * max_sel + first:g * max_sel + first + SAMPLE_V_GROUP]
                vg = vg.reshape(SAMPLE_V_GROUP * MOBA_BLOCK, HEAD_DIM)
                acc = acc + jnp.dot(pg.astype(BF16), vg.astype(BF16), preferred_element_type=F32)
            o_ref[0, g] = acc / den_ref[g]

    @pl.when(s < n_seq)
    def _():
        pick_and_fetch()


def _attn_sample(page_table, q_s, k_new, v_new, cache_k_rows, cache_v):
    n_seq, n_pages = page_table.shape
    n_q, t_new = q_s.shape[2], k_new.shape[2]
    chunk_keys = SAMPLE_CHUNK_PAGES * PAGE_SIZE
    n_blocks = n_pages * PAGE_SIZE // MOBA_BLOCK
    max_sel = min(n_q * MOBA_TOP_K, n_blocks)
    max_sel = -(-max_sel // SAMPLE_V_GROUP) * SAMPLE_V_GROUP
    per_seq = lambda shape: pl.BlockSpec(
        (1,) + shape, lambda s, pt: (jnp.minimum(s, n_seq - 1), 0, 0, 0))
    grid_spec = pltpu.PrefetchScalarGridSpec(
        num_scalar_prefetch=1,
        grid=(n_seq + 1,),
        in_specs=[per_seq((N_KV_HEADS, n_q, HEAD_DIM)), per_seq((N_KV_HEADS, t_new, HEAD_DIM)),
                  per_seq((N_KV_HEADS, t_new, HEAD_DIM)),
                  pl.BlockSpec(memory_space=pl.ANY), pl.BlockSpec(memory_space=pl.ANY)],
        out_specs=pl.BlockSpec((1, N_KV_HEADS, n_q, HEAD_DIM),
                               lambda s, pt: (jnp.maximum(s - 1, 0), 0, 0, 0)),
        scratch_shapes=[pltpu.VMEM((SAMPLE_RING, chunk_keys * N_KV_HEADS, HEAD_DIM), F32),
                        pltpu.SemaphoreType.DMA((SAMPLE_RING,)),
                        pltpu.VMEM((N_KV_HEADS * max_sel, MOBA_BLOCK, HEAD_DIM), F32),
                        pltpu.SemaphoreType.DMA((N_KV_HEADS,)),
                        pltpu.VMEM((N_KV_HEADS, n_q, n_pages * PAGE_SIZE), F32),
                        pltpu.VMEM((n_blocks, KV_W), F32),
                        pltpu.VMEM((N_KV_HEADS, n_blocks, n_q, MOBA_BLOCK), F32),
                        pltpu.VMEM((N_KV_HEADS, max_sel, n_q, MOBA_BLOCK), F32),
                        pltpu.VMEM((N_KV_HEADS, n_q, HEAD_DIM), F32),
                        pltpu.VMEM((N_KV_HEADS, n_q, HEAD_DIM), F32),
                        pltpu.SMEM((N_KV_HEADS,), jnp.int32)],
    )
    return pl.pallas_call(
        functools.partial(_attn_sample_kernel, n_seq=n_seq, n_pages=n_pages),
        out_shape=jax.ShapeDtypeStruct(q_s.shape, F32),
        grid_spec=grid_spec,
        compiler_params=pltpu.CompilerParams(dimension_semantics=("arbitrary",),
                                             vmem_limit_bytes=VMEM_LIMIT_BYTES),
        name="attn_sample",
    )(page_table, q_s, k_new, v_new, cache_k_rows, cache_v)


def _conv_sample_kernel(st_ref, u0_ref, u1_ref, u2_ref, u3_ref, wst_ref, wnew_ref, cb_ref, lg_ref,
                        lb_ref, o0_ref, o1_ref, o2_ref, o3_ref):
    st = st_ref[...]
    new = [u0_ref[...], u1_ref[...], u2_ref[...], u3_ref[...]]
    outs = [o0_ref, o1_ref, o2_ref, o3_ref]
    for t in range(len(new)):
        yc = jnp.sum(st * wst_ref[t][None, :, :], axis=1) + cb_ref[...]
        for tk in range(t + 1):
            yc = yc + new[tk] * wnew_ref[t, tk:tk + 1, :]
        outs[t][...] = _ln_swish(yc, lg_ref[...], lb_ref[...]).astype(outs[t].dtype)


def _conv_sample(state, u_new, conv_w, conv_b, ln_g, ln_b):
    n_seq = state.shape[0]
    t_new = len(u_new)
    ts = 16
    w_state = jnp.stack([jnp.concatenate([jnp.zeros((t, D_MODEL), F32), conv_w[:CONV_STATE - t]], axis=0)
                         for t in range(t_new)])
    w_new = jnp.stack([jnp.concatenate([conv_w[CONV_STATE - t:], jnp.zeros((t_new - 1 - t, D_MODEL), F32)],
                                       axis=0) for t in range(t_new)])
    seqs = pl.BlockSpec((ts, D_MODEL), lambda i: (i, 0))
    return pl.pallas_call(
        _conv_sample_kernel,
        out_shape=tuple(jax.ShapeDtypeStruct((n_seq, D_MODEL), BF16) for _ in range(t_new)),
        grid=(n_seq // ts,),
        in_specs=[pl.BlockSpec((ts, CONV_STATE, D_MODEL), lambda i: (i, 0, 0))] + [seqs] * t_new + [
            _const_spec((t_new, CONV_STATE, D_MODEL)), _const_spec((t_new, t_new, D_MODEL)),
            _const_spec((1, D_MODEL)), _const_spec((1, D_MODEL)), _const_spec((1, D_MODEL))],
        out_specs=tuple(seqs for _ in range(t_new)),
        compiler_params=pltpu.CompilerParams(dimension_semantics=("arbitrary",),
                                             vmem_limit_bytes=VMEM_LIMIT_BYTES),
        name="conv_sample",
    )(state, *u_new, w_state, w_new, conv_b, ln_g, ln_b)


def _post_kernel(c_ref, sgc_ref, sga_ref, o_ref, x_ref, wco_ref, wo_ref, g2_ref, wfi_ref, wfo_ref,
                 y_ref):
    yc = jnp.dot(c_ref[...], wco_ref[...], preferred_element_type=F32)
    m = sgc_ref[...].astype(F32) * yc + sga_ref[...].astype(F32) * o_ref[...].astype(F32)
    x1 = x_ref[...] + jnp.dot(m.astype(BF16), wo_ref[...], preferred_element_type=F32)
    h2 = (x1 * lax.rsqrt(jnp.mean(x1 * x1, axis=-1, keepdims=True) + NORM_EPS) * g2_ref[...]).astype(BF16)
    ab = jnp.dot(h2, wfi_ref[...], preferred_element_type=F32)
    a, b = ab[:, :FFN_HIDDEN], ab[:, FFN_HIDDEN:]
    hid = (a * _sigmoid(a) * b).astype(BF16)
    y_ref[...] = x1 + jnp.dot(hid, wfo_ref[...], preferred_element_type=F32)


def _post(c_act, sgc, sga, o_attn, x2d, wco_bf, wo_bf, g2, wfi_bf, wfo_bf):
    m = x2d.shape[0]
    tm = POST_TILE_ROWS
    rows = pl.BlockSpec((tm, D_MODEL), lambda i: (i, 0))
    return pl.pallas_call(
        _post_kernel,
        out_shape=jax.ShapeDtypeStruct((m, D_MODEL), F32),
        grid=(m // tm,),
        in_specs=[rows, rows, rows, rows, rows,
                  _const_spec((D_MODEL, D_MODEL)), _const_spec((D_MODEL, D_MODEL)),
                  _const_spec((1, D_MODEL)), _const_spec((D_MODEL, 2 * FFN_HIDDEN)),
                  _const_spec((FFN_HIDDEN, D_MODEL))],
        out_specs=rows,
        compiler_params=pltpu.CompilerParams(dimension_semantics=("arbitrary",),
                                             vmem_limit_bytes=VMEM_LIMIT_BYTES),
        name="post",
    )(c_act, sgc, sga, o_attn, x2d, wco_bf, wo_bf, g2, wfi_bf, wfo_bf)


def _rope_tables(pos):
    half = ROT_DIM // 2
    inv_freq = ROPE_THETA ** (-(jnp.arange(half, dtype=F32) * 2.0 / ROT_DIM))
    ang = pos.astype(F32)[:, None] * inv_freq[None, :]
    cos, sin = jnp.cos(ang), jnp.sin(ang)
    n = pos.shape[0]
    zeros = lambda w: jnp.zeros((n, w), F32)
    cos_t = jnp.concatenate([cos, cos, jnp.ones((n, HEAD_DIM - ROT_DIM), F32)], axis=1)
    sa_t = jnp.concatenate([-sin, zeros(HEAD_DIM - half)], axis=1)
    sb_t = jnp.concatenate([zeros(half), sin, zeros(HEAD_DIM - ROT_DIM)], axis=1)
    return cos_t, sa_t, sb_t


def kernel(x_prompt, x_sample, cache_k, cache_v, state_conv, page_table, norm1_g, w_in, q_norm_g,
           k_norm_g, conv_dw_w, conv_dw_b, conv_ln_g, conv_ln_b, w_conv_out, w_out, norm2_g,
           w_ffn_in, w_ffn_out):
    batch, seq, _ = x_prompt.shape
    n_seq, t_new, _ = x_sample.shape
    n_phys = cache_k.shape[0]
    past_len = page_table.shape[1] * PAGE_SIZE
    assert seq % IN_TILE_ROWS == 0 and (n_seq * t_new) % IN_TILE_ROWS == 0
    assert IN_TILE_ROWS % t_new == 0 and past_len % MOBA_BLOCK == 0 and seq >= CONV_STATE

    row = lambda v: v.reshape(1, -1).astype(F32)
    w_in_bf, wco_bf, wo_bf = w_in.astype(BF16), w_conv_out.astype(BF16), w_out.astype(BF16)
    wfi_bf, wfo_bf = w_ffn_in.astype(BF16), w_ffn_out.astype(BF16)
    g1, g2, qg, kg = row(norm1_g), row(norm2_g), row(q_norm_g), row(k_norm_g)
    conv_b, ln_g, ln_b = row(conv_dw_b), row(conv_ln_g), row(conv_ln_b)

    xp = x_prompt.reshape(batch * seq, D_MODEL)
    q_p, k_p, kbf_p, v_p, vbf_p, c_p, sgc_p, sga_p, ksum_p, u_tail_p = _in_proj(
        xp, g1, w_in_bf, qg, kg, *_rope_tables(jnp.arange(seq)),
        conv=(conv_dw_w, conv_b, ln_g, ln_b), seq=seq)
    nb = seq // MOBA_BLOCK
    o_p = _attn_prompt(q_p, kbf_p, vbf_p, ksum_p.reshape(batch, nb, KV_W), batch, seq)
    y_p = _post(c_p, sgc_p, sga_p, o_p, xp, wco_bf, wo_bf, g2, wfi_bf, wfo_bf)

    xs = x_sample.reshape(n_seq * t_new, D_MODEL)
    pos_s = past_len + jnp.arange(IN_TILE_ROWS) % t_new
    q_s, k_s, _, v_s, _, u_s, sgc_s, sga_s, _ = _in_proj(
        xs, g1, w_in_bf, qg, kg, *_rope_tables(pos_s))
    q_g = q_s.reshape(n_seq, t_new, N_KV_HEADS, KV_GROUP, HEAD_DIM).transpose(0, 2, 3, 1, 4)
    q_g = q_g.reshape(n_seq, N_KV_HEADS, KV_GROUP * t_new, HEAD_DIM)
    k_new = k_s.reshape(n_seq, t_new, N_KV_HEADS, HEAD_DIM).transpose(0, 2, 1, 3)
    v_new = v_s.reshape(n_seq, t_new, N_KV_HEADS, HEAD_DIM).transpose(0, 2, 1, 3)
    page_rows = PAGE_SIZE * N_KV_HEADS
    o_g = _attn_sample(page_table, q_g, k_new, v_new, cache_k.reshape(n_phys, page_rows, HEAD_DIM),
                       cache_v)
    o_s = o_g.reshape(n_seq, N_KV_HEADS, KV_GROUP, t_new, HEAD_DIM).transpose(0, 3, 1, 2, 4)
    o_s = o_s.reshape(n_seq * t_new, D_MODEL).astype(BF16)
    u_s3 = u_s.reshape(n_seq, t_new, D_MODEL)
    c_steps = _conv_sample(state_conv, [u_s3[:, t] for t in range(t_new)], conv_dw_w, conv_b, ln_g, ln_b)
    c_s = jnp.stack(c_steps, axis=1).reshape(n_seq * t_new, D_MODEL)
    y_s = _post(c_s, sgc_s, sga_s, o_s, xs, wco_bf, wo_bf, g2, wfi_bf, wfo_bf)

    kv_shape_p = (batch, seq, N_KV_HEADS, HEAD_DIM)
    kv_shape_s = (n_seq, t_new, N_KV_HEADS, HEAD_DIM)
    conv_prompt_state = u_tail_p[:, CONV_HALO_ROWS - CONV_STATE:]
    conv_sample_state = jnp.concatenate([state_conv, u_s3], axis=1)[:, t_new:]
    return (y_p.reshape(batch, seq, D_MODEL), y_s.reshape(n_seq, t_new, D_MODEL),
            k_p.reshape(kv_shape_p), v_p.reshape(kv_shape_p), conv_prompt_state,
            k_s.reshape(kv_shape_s), v_s.reshape(kv_shape_s), conv_sample_state)
```

```python
import functools

import jax
import jax.numpy as jnp
from jax import lax
from jax.experimental import pallas as pl
from jax.experimental.pallas import tpu as pltpu

F32 = jnp.float32
BF16 = jnp.bfloat16

D_MODEL = 1024
N_HEADS = 8
N_KV_HEADS = 4
HEAD_DIM = 128
KV_GROUP = N_HEADS // N_KV_HEADS
ROT_DIM = HEAD_DIM // 4
ROPE_THETA = 500000.0
MOBA_BLOCK = 256
MOBA_SHIFT = MOBA_BLOCK.bit_length() - 1
ATTN_BLOCK_GROUPS = (8, 4, 2, 1)
MOBA_TOP_K = 3
PAGE_SIZE = 128
CONV_WIDTH = 31
CONV_STATE = CONV_WIDTH - 1
FFN_HIDDEN = 2816
NORM_EPS = 1e-6
NEG_INF = -1e30
Q_W = N_HEADS * HEAD_DIM
KV_W = N_KV_HEADS * HEAD_DIM
IN_W = Q_W + 2 * KV_W + 4 * D_MODEL
ATTN_SCALE = HEAD_DIM ** -0.5
LOG2_E = 1.4426950408889634
Q_PRESCALE = ATTN_SCALE * LOG2_E

VMEM_LIMIT_BYTES = 56 * 1024 * 1024

IN_TILE_ROWS = 512
POST_TILE_ROWS = 256
CONV_TILE_ROWS = 128
CONV_HALO_ROWS = 32
CONV_ROW_CHUNK = 64
SUBLANES = 8
SAMPLE_CHUNK_PAGES = 8
SAMPLE_RING = 4
SAMPLE_V_GROUP = 4

NT_DIMS = (((1,), (1,)), ((), ()))


def _sigmoid(x):
    return 1.0 / (1.0 + jnp.exp(-x))


def _const_spec(shape):
    zeros = (0,) * len(shape)
    return pl.BlockSpec(shape, lambda *_: zeros, pipeline_mode=pl.Buffered(1))


def _ln_swish(yc, g, b):
    xc = yc - jnp.mean(yc, axis=-1, keepdims=True)
    var = jnp.mean(xc * xc, axis=-1, keepdims=True)
    z = xc * lax.rsqrt(var + NORM_EPS) * g + b
    return z * _sigmoid(z)


def _causal_conv_tile(ext_ref, sh_ref, yc_ref, base, w_ref, cb_ref):
    lead = CONV_HALO_ROWS - CONV_STATE
    sh_rows = sh_ref.shape[1]
    for ph in range(1, SUBLANES):
        sh_ref[ph - 1] = ext_ref[base + ph:base + ph + sh_rows, :]
    for rc in range(CONV_TILE_ROWS // CONV_ROW_CHUNK):
        for cc in range(D_MODEL // HEAD_DIM):
            lanes = slice(cc * HEAD_DIM, (cc + 1) * HEAD_DIM)
            acc = jnp.broadcast_to(cb_ref[:, lanes], (CONV_ROW_CHUNK, HEAD_DIM))
            for j in range(CONV_WIDTH):
                ph = (lead + j) % SUBLANES
                r0 = rc * CONV_ROW_CHUNK + (lead + j) - ph
                rows = (ext_ref[base + r0:base + r0 + CONV_ROW_CHUNK, lanes] if ph == 0
                        else sh_ref[ph - 1, r0:r0 + CONV_ROW_CHUNK, lanes])
                acc = acc + w_ref[j:j + 1, lanes] * rows
            yc_ref[rc * CONV_ROW_CHUNK:(rc + 1) * CONV_ROW_CHUNK, lanes] = acc


def _in_proj_kernel(*refs, fuse_conv, tiles_per_seq):
    x_ref, g1_ref, w_ref, qg_ref, kg_ref, cos_ref, sa_ref, sb_ref = refs[:8]
    if fuse_conv:
        cw_ref, cb_ref, lg_ref, lb_ref = refs[8:12]
        (q_ref, k32_ref, kbf_ref, v32_ref, vbf_ref, c_ref, sgc_ref, sga_ref, ksum_ref, tail_out_ref,
         ext_ref, sh_ref, yc_ref) = refs[12:]
    else:
        q_ref, k32_ref, kbf_ref, v32_ref, vbf_ref, u_ref, sgc_ref, sga_ref, ksum_ref = refs[8:]
    tm = x_ref.shape[0]
    x = x_ref[...]
    h = (x * lax.rsqrt(jnp.mean(x * x, axis=-1, keepdims=True) + NORM_EPS) * g1_ref[...]).astype(BF16)
    cos, sa, sb = cos_ref[...], sa_ref[...], sb_ref[...]
    half = ROT_DIM // 2

    def norm_rope(z, g):
        zn = z * lax.rsqrt(jnp.mean(z * z, axis=-1, keepdims=True) + NORM_EPS) * g
        return zn * cos + pltpu.roll(zn, HEAD_DIM - half, 1) * sa + pltpu.roll(zn, half, 1) * sb

    def proj(c0, width):
        return jnp.dot(h, w_ref[:, c0:c0 + width], preferred_element_type=F32)

    c0 = Q_W + 2 * KV_W
    u = proj(c0, D_MODEL) * _sigmoid(proj(c0 + D_MODEL, D_MODEL))
    if fuse_conv:
        first = pl.program_id(0) % tiles_per_seq == 0

        @pl.when(first)
        def _():
            ext_ref[0:CONV_HALO_ROWS, :] = jnp.zeros((CONV_HALO_ROWS, D_MODEL), F32)

        @pl.when(jnp.logical_not(first))
        def _():
            ext_ref[0:CONV_HALO_ROWS, :] = ext_ref[tm:tm + CONV_HALO_ROWS, :]

        ext_ref[CONV_HALO_ROWS:, :] = u
        tail_out_ref[0] = u[tm - CONV_HALO_ROWS:]
        for sub in range(tm // CONV_TILE_ROWS):
            _causal_conv_tile(ext_ref, sh_ref, yc_ref, sub * CONV_TILE_ROWS, cw_ref, cb_ref)
            c_ref[sub * CONV_TILE_ROWS:(sub + 1) * CONV_TILE_ROWS, :] = _ln_swish(
                yc_ref[...], lg_ref[...], lb_ref[...]).astype(c_ref.dtype)
    else:
        u_ref[...] = u

    zq = proj(0, Q_W)
    qg = qg_ref[...]
    for hh in range(N_HEADS):
        sl = slice(hh * HEAD_DIM, (hh + 1) * HEAD_DIM)
        q_ref[:, sl] = (norm_rope(zq[:, sl], qg) * Q_PRESCALE).astype(BF16)

    zkv = proj(Q_W, 2 * KV_W)
    kg = kg_ref[...]
    n_blk = ksum_ref.shape[1]
    for g in range(N_KV_HEADS):
        sl = slice(g * HEAD_DIM, (g + 1) * HEAD_DIM)
        kk = norm_rope(zkv[:, sl], kg)
        vv = zkv[:, KV_W + g * HEAD_DIM:KV_W + (g + 1) * HEAD_DIM]
        k32_ref[pl.ds(g, tm, stride=N_KV_HEADS), :] = kk
        v32_ref[pl.ds(g, tm, stride=N_KV_HEADS), :] = vv
        kbf_ref[:, sl] = kk.astype(BF16)
        vbf_ref[:, sl] = vv.astype(BF16)
        for blk in range(n_blk):
            ksum_ref[0, blk:blk + 1, sl] = jnp.sum(
                kk[blk * MOBA_BLOCK:(blk + 1) * MOBA_BLOCK], axis=0, keepdims=True)

    sgc_ref[...] = _sigmoid(proj(c0 + 2 * D_MODEL, D_MODEL)).astype(BF16)
    sga_ref[...] = _sigmoid(proj(c0 + 3 * D_MODEL, D_MODEL)).astype(BF16)


def _in_proj(x2d, g1, w_bf, qg, kg, cos_t, sa_t, sb_t, conv=None, seq=None):
    m = x2d.shape[0]
    tm = IN_TILE_ROWS
    nt = m // tm
    ntab = cos_t.shape[0] // tm
    n_blk = tm // MOBA_BLOCK
    fuse_conv = conv is not None
    row = lambda i: (i, 0)
    tab = lambda i: (i % ntab, 0)
    rows = lambda w: pl.BlockSpec((tm, w), row)
    head_rows = pl.BlockSpec((tm * N_KV_HEADS, HEAD_DIM), row)
    out_shape = [
        jax.ShapeDtypeStruct((m, Q_W), BF16),
        jax.ShapeDtypeStruct((m * N_KV_HEADS, HEAD_DIM), F32),
        jax.ShapeDtypeStruct((m, KV_W), BF16),
        jax.ShapeDtypeStruct((m * N_KV_HEADS, HEAD_DIM), F32),
        jax.ShapeDtypeStruct((m, KV_W), BF16),
        jax.ShapeDtypeStruct((m, D_MODEL), BF16 if fuse_conv else F32),
        jax.ShapeDtypeStruct((m, D_MODEL), BF16),
        jax.ShapeDtypeStruct((m, D_MODEL), BF16),
        jax.ShapeDtypeStruct((nt, n_blk, KV_W), F32),
    ]
    in_specs = [rows(D_MODEL), _const_spec((1, D_MODEL)), _const_spec((D_MODEL, IN_W)),
                _const_spec((1, HEAD_DIM)), _const_spec((1, HEAD_DIM)),
                pl.BlockSpec((tm, HEAD_DIM), tab), pl.BlockSpec((tm, HEAD_DIM), tab),
                pl.BlockSpec((tm, HEAD_DIM), tab)]
    out_specs = [rows(Q_W), head_rows, rows(KV_W), head_rows, rows(KV_W), rows(D_MODEL),
                 rows(D_MODEL), rows(D_MODEL), pl.BlockSpec((1, n_blk, KV_W), lambda i: (i, 0, 0))]
    args = [x2d, g1, w_bf, qg, kg, cos_t, sa_t, sb_t]
    scratch_shapes = []
    tiles_per_seq = None
    if fuse_conv:
        tiles_per_seq = seq // tm
        in_specs += [_const_spec((CONV_WIDTH, D_MODEL))] + [_const_spec((1, D_MODEL))] * 3
        args += list(conv)
        out_shape.append(jax.ShapeDtypeStruct((m // seq, CONV_HALO_ROWS, D_MODEL), F32))
        out_specs.append(pl.BlockSpec((1, CONV_HALO_ROWS, D_MODEL), lambda i: (i // tiles_per_seq, 0, 0)))
        scratch_shapes = [
            pltpu.VMEM((tm + CONV_HALO_ROWS, D_MODEL), F32),
            pltpu.VMEM((SUBLANES - 1, CONV_TILE_ROWS + CONV_HALO_ROWS - SUBLANES, D_MODEL), F32),
            pltpu.VMEM((CONV_TILE_ROWS, D_MODEL), F32)]
    return pl.pallas_call(
        functools.partial(_in_proj_kernel, fuse_conv=fuse_conv, tiles_per_seq=tiles_per_seq),
        out_shape=tuple(out_shape),
        grid=(nt,),
        in_specs=in_specs,
        out_specs=tuple(out_specs),
        scratch_shapes=scratch_shapes,
        compiler_params=pltpu.CompilerParams(dimension_semantics=("arbitrary",),
                                             vmem_limit_bytes=VMEM_LIMIT_BYTES),
        name="in_proj",
    )(*args)


def _split3_bf16(a):
    a1 = a.astype(BF16)
    r1 = a - a1.astype(F32)
    a2 = r1.astype(BF16)
    a3 = (r1 - a2.astype(F32)).astype(BF16)
    return a1, a2, a3


def _block_scores(q_bf, means, blocks_first=False):
    if blocks_first:
        return sum(lax.dot_general(part, q_bf, NT_DIMS, preferred_element_type=F32)
                   for part in _split3_bf16(means))
    return sum(lax.dot_general(q_bf, part, NT_DIMS, preferred_element_type=F32)
               for part in _split3_bf16(means))


def _select_bias(scores, n_elig, block_axis):
    nb = scores.shape[block_axis]
    blk = lax.broadcasted_iota(jnp.int32, scores.shape, block_axis)
    sc = jnp.where(blk < n_elig, scores, NEG_INF)
    rank = jnp.zeros(scores.shape, F32)
    for c in range(nb):
        s_c = sc[c:c + 1, :] if block_axis == 0 else sc[:, c:c + 1]
        tie = jnp.where(blk > c, 1.0, 0.0)
        rank = rank + jnp.where(s_c > sc, 1.0, jnp.where(s_c == sc, tie, 0.0))
    keep = jnp.where(blk < n_elig, jnp.where(rank < MOBA_TOP_K, 1.0, 0.0), 0.0)
    return jnp.where(keep > 0.5, 0.0, NEG_INF)


def _attn_prompt_kernel(q_ref, k_ref, v_ref, ks_ref, o_ref, s_ref):
    i = pl.program_id(2)
    tq = q_ref.shape[0]
    nb = ks_ref.shape[1]
    q2 = jnp.concatenate([q_ref[:, :HEAD_DIM], q_ref[:, HEAD_DIM:]], axis=0)
    rows = q2.shape[0]
    means = ks_ref[0] * (1.0 / MOBA_BLOCK)
    bias_t = _select_bias(_block_scores(q2, means, blocks_first=True), i, 0)
    bias_pad = jnp.concatenate([bias_t, jnp.zeros((HEAD_DIM - nb, rows), F32)], axis=0)
    q_aug = jnp.concatenate([q2, bias_pad.T.astype(BF16)], axis=1)

    def block_start(j):
        return pl.multiple_of(j * MOBA_BLOCK, MOBA_BLOCK)

    def lane_max(mx, s):
        return jnp.maximum(mx, jnp.maximum(s[:, :HEAD_DIM], s[:, HEAD_DIM:]))

    def sweep(n, group_body, carry):
        done = 0
        for u in ATTN_BLOCK_GROUPS:
            count = (n - done) // u
            carry = lax.fori_loop(
                0, count, lambda t, c, done=done, u=u: group_body(done + t * u, u, c), carry)
            done = done + count * u
        return carry

    def past_logits(j0, u, mx):
        kk = k_ref[pl.ds(block_start(j0), u * MOBA_BLOCK), :]
        shape = (u * MOBA_BLOCK, HEAD_DIM)
        blk = j0 + (lax.broadcasted_iota(jnp.int32, shape, 0) >> MOBA_SHIFT)
        lane = lax.broadcasted_iota(jnp.int32, shape, 1)
        pick = jnp.where(lane == blk, 1.0, 0.0).astype(BF16)
        s = lax.dot_general(q_aug, jnp.concatenate([kk, pick], axis=1), NT_DIMS,
                            preferred_element_type=F32)
        for w in range(u):
            sw = s[:, w * MOBA_BLOCK:(w + 1) * MOBA_BLOCK]
            s_ref[j0 + w] = sw
            mx = lane_max(mx, sw)
        return mx

    ki = k_ref[pl.ds(block_start(i), MOBA_BLOCK), :]
    s = lax.dot_general(q2, ki, NT_DIMS, preferred_element_type=F32)
    t = lax.broadcasted_iota(jnp.int32, s.shape, 0) % tq
    kpos = lax.broadcasted_iota(jnp.int32, s.shape, 1)
    s = jnp.where(kpos <= t, s, NEG_INF)
    s_ref[i] = s
    mx = lane_max(jnp.full((rows, HEAD_DIM), NEG_INF, F32), s)

    mx = sweep(i, past_logits, mx)
    m = jnp.broadcast_to(jnp.max(mx, axis=-1, keepdims=True), (rows, HEAD_DIM))

    def weighted_values(j0, u, acc):
        parts = []
        for w in range(u):
            sw = s_ref[j0 + w]
            parts += [jnp.exp2(sw[:, :HEAD_DIM] - m), jnp.exp2(sw[:, HEAD_DIM:] - m)]
        p = jnp.concatenate(parts, axis=1).astype(BF16)
        vv = v_ref[pl.ds(block_start(j0), u * MOBA_BLOCK), :]
        ones = jnp.ones((u * MOBA_BLOCK, HEAD_DIM), BF16)
        return acc + jnp.dot(p, jnp.concatenate([vv, ones], axis=1), preferred_element_type=F32)

    acc = sweep(i + 1, weighted_values, jnp.zeros((rows, 2 * HEAD_DIM), F32))
    out = acc[:, :HEAD_DIM] / acc[:, HEAD_DIM:]
    o_ref[:, :HEAD_DIM] = out[:tq].astype(o_ref.dtype)
    o_ref[:, HEAD_DIM:] = out[tq:].astype(o_ref.dtype)


def _attn_prompt(q_bf, k_bf, v_bf, ksum, batch, seq):
    nb = seq // MOBA_BLOCK
    gw = KV_GROUP * HEAD_DIM
    return pl.pallas_call(
        _attn_prompt_kernel,
        out_shape=jax.ShapeDtypeStruct(q_bf.shape, BF16),
        grid=(batch, N_KV_HEADS, nb),
        in_specs=[pl.BlockSpec((MOBA_BLOCK, gw), lambda b, g, i: (b * nb + i, g)),
                  pl.BlockSpec((seq, HEAD_DIM), lambda b, g, i: (b, g)),
                  pl.BlockSpec((seq, HEAD_DIM), lambda b, g, i: (b, g)),
                  pl.BlockSpec((1, nb, HEAD_DIM), lambda b, g, i: (b, 0, g))],
        out_specs=pl.BlockSpec((MOBA_BLOCK, gw), lambda b, g, i: (b * nb + i, g)),
        scratch_shapes=[pltpu.VMEM((nb, KV_GROUP * MOBA_BLOCK, 2 * HEAD_DIM), F32)],
        compiler_params=pltpu.CompilerParams(
            dimension_semantics=("arbitrary", "arbitrary", "arbitrary"),
            vmem_limit_bytes=VMEM_LIMIT_BYTES),
        name="attn_prompt",
    )(q_bf, k_bf, v_bf, ksum)


def _attn_sample_kernel(pt_ref, q_ref, kn_ref, vn_ref, ck_hbm, cv_hbm, o_ref,
                        ring, sems, vbuf, vsems, logit_ref, sum_ref, p_ref, pc_ref, own_ref, den_ref,
                        nsel_ref, *, n_seq, n_pages):
    s = pl.program_id(0)
    cp = SAMPLE_CHUNK_PAGES
    n_chunks = n_pages // cp
    chunk_keys = cp * PAGE_SIZE
    blk_per_chunk = chunk_keys // MOBA_BLOCK
    pages_per_blk = MOBA_BLOCK // PAGE_SIZE
    n_blocks = n_pages * PAGE_SIZE // MOBA_BLOCK
    n_q = q_ref.shape[2]
    t_new = kn_ref.shape[2]
    page_rows = PAGE_SIZE * N_KV_HEADS
    max_sel = pc_ref.shape[1]
    assert n_chunks % SAMPLE_RING == 0 and n_blocks <= 32 and max_sel % SAMPLE_V_GROUP == 0

    def page_copy(seq, c, r):
        page = pt_ref[seq, c * cp + r]
        slot = c % SAMPLE_RING
        return pltpu.make_async_copy(ck_hbm.at[page], ring.at[slot, pl.ds(r * page_rows, page_rows)],
                                     sems.at[slot])

    def v_copy(g, blk, slot, r):
        page = pt_ref[s, blk * pages_per_blk + r]
        return pltpu.make_async_copy(cv_hbm.at[page, :, g, :],
                                     vbuf.at[g * max_sel + slot, pl.ds(r * PAGE_SIZE, PAGE_SIZE)],
                                     vsems.at[g])

    def head_rows(slot, g):
        return ring.at[slot][pl.ds(g, chunk_keys, stride=N_KV_HEADS), :]

    def start_chunk(seq, c):
        for r in range(cp):
            page_copy(seq, c, r).start()

    def wait_chunk(seq, c):
        for r in range(cp):
            page_copy(seq, c, r).wait()

    depth = SAMPLE_RING - 1

    @pl.when(s == 0)
    def _():
        for c in range(depth):
            start_chunk(s, c)

        def clear(t, carry):
            vbuf[t] = jnp.zeros(vbuf.shape[1:], F32)
            return carry

        lax.fori_loop(0, vbuf.shape[0], clear, 0)

    def pick_and_fetch():
        qs = [q_ref[0, g] for g in range(N_KV_HEADS)]
        scores = jnp.concatenate(
            [_block_scores(qs[g], sum_ref[:, g * HEAD_DIM:(g + 1) * HEAD_DIM] * (1.0 / MOBA_BLOCK))
             for g in range(N_KV_HEADS)], axis=0)
        bias_all = _select_bias(scores, n_blocks, 1)
        picked = jnp.where(bias_all == 0.0, 1.0, 0.0)
        e_shape = (n_blocks, n_blocks * MOBA_BLOCK)
        expand = jnp.where(
            (lax.broadcasted_iota(jnp.int32, e_shape, 1) >> MOBA_SHIFT)
            == lax.broadcasted_iota(jnp.int32, e_shape, 0), 1.0, 0.0).astype(BF16)
        blk_lane = lax.broadcasted_iota(jnp.int32, (1, n_blocks), 1)
        half_weight = jnp.left_shift(1, blk_lane & 15).astype(F32)
        low_weight = jnp.where(blk_lane < 16, half_weight, 0.0)
        high_weight = jnp.where(blk_lane >= 16, half_weight, 0.0)

        def to_scalar(v):
            return jnp.sum(v, axis=-1, keepdims=True).astype(jnp.int32)[0, 0]

        masks = []
        for g in range(N_KV_HEADS):
            rows_g = slice(g * n_q, (g + 1) * n_q)
            any_row = jnp.max(picked[rows_g], axis=0, keepdims=True)
            masks.append(to_scalar(any_row * low_weight) | (to_scalar(any_row * high_weight) << 16))
            bias = jnp.dot(bias_all[rows_g].astype(BF16), expand, preferred_element_type=F32)
            qf = qs[g].astype(F32)
            kn = kn_ref[0, g]
            trow = lax.broadcasted_iota(jnp.int32, (n_q, 1), 0) % t_new
            own = []
            for tk in range(t_new):
                lo = jnp.sum(qf * kn[tk:tk + 1, :], axis=-1, keepdims=True)
                own.append(jnp.where(trow >= tk, lo, NEG_INF))
            mx = own[0]
            for lo in own[1:]:
                mx = jnp.maximum(mx, lo)
            lm = logit_ref[g] + bias
            mx = jnp.maximum(mx, jnp.max(lm, axis=-1, keepdims=True))
            p = jnp.exp2(lm - mx)
            den = jnp.sum(p, axis=-1, keepdims=True)
            vn = vn_ref[0, g]
            own_out = jnp.zeros((n_q, HEAD_DIM), F32)
            for tk in range(t_new):
                p_own = jnp.exp2(own[tk] - mx)
                den = den + p_own
                own_out = own_out + p_own * vn[tk:tk + 1, :]
            own_ref[g] = own_out
            den_ref[g] = jnp.broadcast_to(den, (n_q, HEAD_DIM))
            for b in range(n_blocks):
                p_ref[g, b] = p[:, b * MOBA_BLOCK:(b + 1) * MOBA_BLOCK]
            pc_ref[g] = jnp.zeros(pc_ref.shape[1:], F32)

        for g in range(N_KV_HEADS):
            def issue(blk, count, g=g):
                bit = (masks[g] >> blk) & 1

                @pl.when(bit == 1)
                def _():
                    for r in range(pages_per_blk):
                        v_copy(g, blk, count, r).start()
                    pc_ref[g, count] = p_ref[g, blk]

                return count + bit

            nsel_ref[g] = lax.fori_loop(0, n_blocks, issue, jnp.int32(0))

    def stream_keys():
        qs = [q_ref[0, g] for g in range(N_KV_HEADS)]
        for c in range(n_chunks):
            wait_chunk(s, c)
            nxt = c + depth
            if nxt < n_chunks:
                start_chunk(s, nxt)
            else:
                @pl.when(s + 1 < n_seq)
                def _():
                    start_chunk(s + 1, nxt - n_chunks)
            slot = c % SAMPLE_RING
            for g in range(N_KV_HEADS):
                kc = head_rows(slot, g)
                for b in range(blk_per_chunk):
                    row = c * blk_per_chunk + b
                    sum_ref[row:row + 1, g * HEAD_DIM:(g + 1) * HEAD_DIM] = jnp.sum(
                        kc[b * MOBA_BLOCK:(b + 1) * MOBA_BLOCK], axis=0, keepdims=True)
                logit_ref[g, :, c * chunk_keys:(c + 1) * chunk_keys] = lax.dot_general(
                    qs[g], kc.astype(BF16), NT_DIMS, preferred_element_type=F32)

    def weighted_values():
        for g in range(N_KV_HEADS):
            def wait_block(t, carry, g=g):
                for r in range(pages_per_blk):
                    v_copy(g, 0, 0, r).wait()
                return carry

            lax.fori_loop(0, nsel_ref[g], wait_block, 0)
            acc = own_ref[g]
            for t in range(max_sel // SAMPLE_V_GROUP):
                first = t * SAMPLE_V_GROUP
                pg = jnp.concatenate([pc_ref[g, first + w] for w in range(SAMPLE_V_GROUP)], axis=1)
                vg = vbuf[g * max_sel + first:g * max_sel + first + SAMPLE_V_GROUP]
                vg = vg.reshape(SAMPLE_V_GROUP * MOBA_BLOCK, HEAD_DIM)
                acc = acc + jnp.dot(pg.astype(BF16), vg.astype(BF16), preferred_element_type=F32)
            o_ref[0, g] = acc / den_ref[g]

    stream_keys()
    pick_and_fetch()
    weighted_values()


def _attn_sample(page_table, q_s, k_new, v_new, cache_k_rows, cache_v):
    n_seq, n_pages = page_table.shape
    n_q, t_new = q_s.shape[2], k_new.shape[2]
    chunk_keys = SAMPLE_CHUNK_PAGES * PAGE_SIZE
    n_blocks = n_pages * PAGE_SIZE // MOBA_BLOCK
    max_sel = min(n_q * MOBA_TOP_K, n_blocks)
    max_sel = -(-max_sel // SAMPLE_V_GROUP) * SAMPLE_V_GROUP
    per_seq = lambda shape: pl.BlockSpec((1,) + shape, lambda s, pt: (s, 0, 0, 0))
    grid_spec = pltpu.PrefetchScalarGridSpec(
        num_scalar_prefetch=1,
        grid=(n_seq,),
        in_specs=[per_seq((N_KV_HEADS, n_q, HEAD_DIM)), per_seq((N_KV_HEADS, t_new, HEAD_DIM)),
                  per_seq((N_KV_HEADS, t_new, HEAD_DIM)),
                  pl.BlockSpec(memory_space=pl.ANY), pl.BlockSpec(memory_space=pl.ANY)],
        out_specs=per_seq((N_KV_HEADS, n_q, HEAD_DIM)),
        scratch_shapes=[pltpu.VMEM((SAMPLE_RING, chunk_keys * N_KV_HEADS, HEAD_DIM), F32),
                        pltpu.SemaphoreType.DMA((SAMPLE_RING,)),
                        pltpu.VMEM((N_KV_HEADS * max_sel, MOBA_BLOCK, HEAD_DIM), F32),
                        pltpu.SemaphoreType.DMA((N_KV_HEADS,)),
                        pltpu.VMEM((N_KV_HEADS, n_q, n_pages * PAGE_SIZE), F32),
                        pltpu.VMEM((n_blocks, KV_W), F32),
                        pltpu.VMEM((N_KV_HEADS, n_blocks, n_q, MOBA_BLOCK), F32),
                        pltpu.VMEM((N_KV_HEADS, max_sel, n_q, MOBA_BLOCK), F32),
                        pltpu.VMEM((N_KV_HEADS, n_q, HEAD_DIM), F32),
                        pltpu.VMEM((N_KV_HEADS, n_q, HEAD_DIM), F32),
                        pltpu.SMEM((N_KV_HEADS,), jnp.int32)],
    )
    return pl.pallas_call(
        functools.partial(_attn_sample_kernel, n_seq=n_seq, n_pages=n_pages),
        out_shape=jax.ShapeDtypeStruct(q_s.shape, F32),
        grid_spec=grid_spec,
        compiler_params=pltpu.CompilerParams(dimension_semantics=("arbitrary",),
                                             vmem_limit_bytes=VMEM_LIMIT_BYTES),
        name="attn_sample",
    )(page_table, q_s, k_new, v_new, cache_k_rows, cache_v)


def _conv_sample_kernel(st_ref, u0_ref, u1_ref, u2_ref, u3_ref, wst_ref, wnew_ref, cb_ref, lg_ref,
                        lb_ref, o0_ref, o1_ref, o2_ref, o3_ref):
    st = st_ref[...]
    new = [u0_ref[...], u1_ref[...], u2_ref[...], u3_ref[...]]
    outs = [o0_ref, o1_ref, o2_ref, o3_ref]
    for t in range(len(new)):
        yc = jnp.sum(st * wst_ref[t][None, :, :], axis=1) + cb_ref[...]
        for tk in range(t + 1):
            yc = yc + new[tk] * wnew_ref[t, tk:tk + 1, :]
        outs[t][...] = _ln_swish(yc, lg_ref[...], lb_ref[...]).astype(outs[t].dtype)


def _conv_sample(state, u_new, conv_w, conv_b, ln_g, ln_b):
    n_seq = state.shape[0]
    t_new = len(u_new)
    ts = 16
    w_state = jnp.stack([jnp.concatenate([jnp.zeros((t, D_MODEL), F32), conv_w[:CONV_STATE - t]], axis=0)
                         for t in range(t_new)])
    w_new = jnp.stack([jnp.concatenate([conv_w[CONV_STATE - t:], jnp.zeros((t_new - 1 - t, D_MODEL), F32)],
                                       axis=0) for t in range(t_new)])
    seqs = pl.BlockSpec((ts, D_MODEL), lambda i: (i, 0))
    return pl.pallas_call(
        _conv_sample_kernel,
        out_shape=tuple(jax.ShapeDtypeStruct((n_seq, D_MODEL), BF16) for _ in range(t_new)),
        grid=(n_seq // ts,),
        in_specs=[pl.BlockSpec((ts, CONV_STATE, D_MODEL), lambda i: (i, 0, 0))] + [seqs] * t_new + [
            _const_spec((t_new, CONV_STATE, D_MODEL)), _const_spec((t_new, t_new, D_MODEL)),
            _const_spec((1, D_MODEL)), _const_spec((1, D_MODEL)), _const_spec((1, D_MODEL))],
        out_specs=tuple(seqs for _ in range(t_new)),
        compiler_params=pltpu.CompilerParams(dimension_semantics=("arbitrary",),
                                             vmem_limit_bytes=VMEM_LIMIT_BYTES),
        name="conv_sample",
    )(state, *u_new, w_state, w_new, conv_b, ln_g, ln_b)


def _post_kernel(c_ref, sgc_ref, sga_ref, o_ref, x_ref, wco_ref, wo_ref, g2_ref, wfi_ref, wfo_ref,
                 y_ref):
    yc = jnp.dot(c_ref[...], wco_ref[...], preferred_element_type=F32)
    m = sgc_ref[...].astype(F32) * yc + sga_ref[...].astype(F32) * o_ref[...].astype(F32)
    x1 = x_ref[...] + jnp.dot(m.astype(BF16), wo_ref[...], preferred_element_type=F32)
    h2 = (x1 * lax.rsqrt(jnp.mean(x1 * x1, axis=-1, keepdims=True) + NORM_EPS) * g2_ref[...]).astype(BF16)
    ab = jnp.dot(h2, wfi_ref[...], preferred_element_type=F32)
    a, b = ab[:, :FFN_HIDDEN], ab[:, FFN_HIDDEN:]
    hid = (a * _sigmoid(a) * b).astype(BF16)
    y_ref[...] = x1 + jnp.dot(hid, wfo_ref[...], preferred_element_type=F32)


def _post(c_act, sgc, sga, o_attn, x2d, wco_bf, wo_bf, g2, wfi_bf, wfo_bf):
    m = x2d.shape[0]
    tm = POST_TILE_ROWS
    rows = pl.BlockSpec((tm, D_MODEL), lambda i: (i, 0))
    return pl.pallas_call(
        _post_kernel,
        out_shape=jax.ShapeDtypeStruct((m, D_MODEL), F32),
        grid=(m // tm,),
        in_specs=[rows, rows, rows, rows, rows,
                  _const_spec((D_MODEL, D_MODEL)), _const_spec((D_MODEL, D_MODEL)),
                  _const_spec((1, D_MODEL)), _const_spec((D_MODEL, 2 * FFN_HIDDEN)),
                  _const_spec((FFN_HIDDEN, D_MODEL))],
        out_specs=rows,
        compiler_params=pltpu.CompilerParams(dimension_semantics=("arbitrary",),
                                             vmem_limit_bytes=VMEM_LIMIT_BYTES),
        name="post",
    )(c_act, sgc, sga, o_attn, x2d, wco_bf, wo_bf, g2, wfi_bf, wfo_bf)


def _rope_tables(pos):
    half = ROT_DIM // 2
    inv_freq = ROPE_THETA ** (-(jnp.arange(half, dtype=F32) * 2.0 / ROT_DIM))
    ang = pos.astype(F32)[:, None] * inv_freq[None, :]
    cos, sin = jnp.cos(ang), jnp.sin(ang)
    n = pos.shape[0]
    zeros = lambda w: jnp.zeros((n, w), F32)
    cos_t = jnp.concatenate([cos, cos, jnp.ones((n, HEAD_DIM - ROT_DIM), F32)], axis=1)
    sa_t = jnp.concatenate([-sin, zeros(HEAD_DIM - half)], axis=1)
    sb_t = jnp.concatenate([zeros(half), sin, zeros(HEAD_DIM - ROT_DIM)], axis=1)
    return cos_t, sa_t, sb_t


def kernel(x_prompt, x_sample, cache_k, cache_v, state_conv, page_table, norm1_g, w_in, q_norm_g,
           k_norm_g, conv_dw_w, conv_dw_b, conv_ln_g, conv_ln_b, w_conv_out, w_out, norm2_g,
           w_ffn_in, w_ffn_out):
    batch, seq, _ = x_prompt.shape
    n_seq, t_new, _ = x_sample.shape
    n_phys = cache_k.shape[0]
    past_len = page_table.shape[1] * PAGE_SIZE
    assert seq % IN_TILE_ROWS == 0 and (n_seq * t_new) % IN_TILE_ROWS == 0
    assert IN_TILE_ROWS % t_new == 0 and past_len % MOBA_BLOCK == 0 and seq >= CONV_STATE

    row = lambda v: v.reshape(1, -1).astype(F32)
    w_in_bf, wco_bf, wo_bf = w_in.astype(BF16), w_conv_out.astype(BF16), w_out.astype(BF16)
    wfi_bf, wfo_bf = w_ffn_in.astype(BF16), w_ffn_out.astype(BF16)
    g1, g2, qg, kg = row(norm1_g), row(norm2_g), row(q_norm_g), row(k_norm_g)
    conv_b, ln_g, ln_b = row(conv_dw_b), row(conv_ln_g), row(conv_ln_b)

    xp = x_prompt.reshape(batch * seq, D_MODEL)
    q_p, k_p, kbf_p, v_p, vbf_p, c_p, sgc_p, sga_p, ksum_p, u_tail_p = _in_proj(
        xp, g1, w_in_bf, qg, kg, *_rope_tables(jnp.arange(seq)),
        conv=(conv_dw_w, conv_b, ln_g, ln_b), seq=seq)
    nb = seq // MOBA_BLOCK
    o_p = _attn_prompt(q_p, kbf_p, vbf_p, ksum_p.reshape(batch, nb, KV_W), batch, seq)
    y_p = _post(c_p, sgc_p, sga_p, o_p, xp, wco_bf, wo_bf, g2, wfi_bf, wfo_bf)

    xs = x_sample.reshape(n_seq * t_new, D_MODEL)
    pos_s = past_len + jnp.arange(IN_TILE_ROWS) % t_new
    q_s, k_s, _, v_s, _, u_s, sgc_s, sga_s, _ = _in_proj(
        xs, g1, w_in_bf, qg, kg, *_rope_tables(pos_s))
    q_g = q_s.reshape(n_seq, t_new, N_KV_HEADS, KV_GROUP, HEAD_DIM).transpose(0, 2, 3, 1, 4)
    q_g = q_g.reshape(n_seq, N_KV_HEADS, KV_GROUP * t_new, HEAD_DIM)
    k_new = k_s.reshape(n_seq, t_new, N_KV_HEADS, HEAD_DIM).transpose(0, 2, 1, 3)
    v_new = v_s.reshape(n_seq, t_new, N_KV_HEADS, HEAD_DIM).transpose(0, 2, 1, 3)
    page_rows = PAGE_SIZE * N_KV_HEADS
    o_g = _attn_sample(page_table, q_g, k_new, v_new, cache_k.reshape(n_phys, page_rows, HEAD_DIM),
                       cache_v)
    o_s = o_g.reshape(n_seq, N_KV_HEADS, KV_GROUP, t_new, HEAD_DIM).transpose(0, 3, 1, 2, 4)
    o_s = o_s.reshape(n_seq * t_new, D_MODEL).astype(BF16)
    u_s3 = u_s.reshape(n_seq, t_new, D_MODEL)
    c_steps = _conv_sample(state_conv, [u_s3[:, t] for t in range(t_new)], conv_dw_w, conv_b, ln_g, ln_b)
    c_s = jnp.stack(c_steps, axis=1).reshape(n_seq * t_new, D_MODEL)
    y_s = _post(c_s, sgc_s, sga_s, o_s, xs, wco_bf, wo_bf, g2, wfi_bf, wfo_bf)

    kv_shape_p = (batch, seq, N_KV_HEADS, HEAD_DIM)
    kv_shape_s = (n_seq, t_new, N_KV_HEADS, HEAD_DIM)
    conv_prompt_state = u_tail_p[:, CONV_HALO_ROWS - CONV_STATE:]
    conv_sample_state = jnp.concatenate([state_conv, u_s3], axis=1)[:, t_new:]
    return (y_p.reshape(batch, seq, D_MODEL), y_s.reshape(n_seq, t_new, D_MODEL),
            k_p.reshape(kv_shape_p), v_p.reshape(kv_shape_p), conv_prompt_state,
            k_s.reshape(kv_shape_s), v_s.reshape(kv_shape_s), conv_sample_state)
```

```python
import functools

import jax
import jax.numpy as jnp
from jax import lax
from jax.experimental import pallas as pl
from jax.experimental.pallas import tpu as pltpu

F32 = jnp.float32
BF16 = jnp.bfloat16

D_MODEL = 1024
N_HEADS = 8
N_KV_HEADS = 4
HEAD_DIM = 128
KV_GROUP = N_HEADS // N_KV_HEADS
ROT_DIM = HEAD_DIM // 4
ROPE_THETA = 500000.0
MOBA_BLOCK = 256
MOBA_SHIFT = MOBA_BLOCK.bit_length() - 1
ATTN_BLOCK_GROUPS = (8, 4, 2, 1)
MOBA_TOP_K = 3
PAGE_SIZE = 128
CONV_WIDTH = 31
CONV_STATE = CONV_WIDTH - 1
FFN_HIDDEN = 2816
NORM_EPS = 1e-6
NEG_INF = -1e30
Q_W = N_HEADS * HEAD_DIM
KV_W = N_KV_HEADS * HEAD_DIM
IN_W = Q_W + 2 * KV_W + 4 * D_MODEL
ATTN_SCALE = HEAD_DIM ** -0.5
LOG2_E = 1.4426950408889634
Q_PRESCALE = ATTN_SCALE * LOG2_E

VMEM_LIMIT_BYTES = 56 * 1024 * 1024

IN_TILE_ROWS = 512
POST_TILE_ROWS = 256
CONV_TILE_ROWS = 128
CONV_HALO_ROWS = 32
CONV_ROW_CHUNK = 64
SUBLANES = 8
SAMPLE_CHUNK_PAGES = 8
SAMPLE_RING = 4
SAMPLE_V_GROUP = 4

NT_DIMS = (((1,), (1,)), ((), ()))


def _sigmoid(x):
    return 1.0 / (1.0 + jnp.exp(-x))


def _const_spec(shape):
    zeros = (0,) * len(shape)
    return pl.BlockSpec(shape, lambda *_: zeros, pipeline_mode=pl.Buffered(1))


def _ln_swish(yc, g, b):
    xc = yc - jnp.mean(yc, axis=-1, keepdims=True)
    var = jnp.mean(xc * xc, axis=-1, keepdims=True)
    z = xc * lax.rsqrt(var + NORM_EPS) * g + b
    return z * _sigmoid(z)


def _causal_conv_tile(ext_ref, sh_ref, yc_ref, base, w_ref, cb_ref):
    lead = CONV_HALO_ROWS - CONV_STATE
    sh_rows = sh_ref.shape[1]
    for ph in range(1, SUBLANES):
        sh_ref[ph - 1] = ext_ref[base + ph:base + ph + sh_rows, :]
    for rc in range(CONV_TILE_ROWS // CONV_ROW_CHUNK):
        for cc in range(D_MODEL // HEAD_DIM):
            lanes = slice(cc * HEAD_DIM, (cc + 1) * HEAD_DIM)
            acc = jnp.broadcast_to(cb_ref[:, lanes], (CONV_ROW_CHUNK, HEAD_DIM))
            for j in range(CONV_WIDTH):
                ph = (lead + j) % SUBLANES
                r0 = rc * CONV_ROW_CHUNK + (lead + j) - ph
                rows = (ext_ref[base + r0:base + r0 + CONV_ROW_CHUNK, lanes] if ph == 0
                        else sh_ref[ph - 1, r0:r0 + CONV_ROW_CHUNK, lanes])
                acc = acc + w_ref[j:j + 1, lanes] * rows
            yc_ref[rc * CONV_ROW_CHUNK:(rc + 1) * CONV_ROW_CHUNK, lanes] = acc


def _in_proj_kernel(*refs, fuse_conv, tiles_per_seq):
    x_ref, g1_ref, w_ref, qg_ref, kg_ref, cos_ref, sa_ref, sb_ref = refs[:8]
    if fuse_conv:
        cw_ref, cb_ref, lg_ref, lb_ref = refs[8:12]
        (q_ref, k32_ref, kbf_ref, v32_ref, vbf_ref, c_ref, sgc_ref, sga_ref, ksum_ref, tail_out_ref,
         ext_ref, sh_ref, yc_ref) = refs[12:]
    else:
        q_ref, k32_ref, kbf_ref, v32_ref, vbf_ref, u_ref, sgc_ref, sga_ref, ksum_ref = refs[8:]
    tm = x_ref.shape[0]
    x = x_ref[...]
    h = (x * lax.rsqrt(jnp.mean(x * x, axis=-1, keepdims=True) + NORM_EPS) * g1_ref[...]).astype(BF16)
    cos, sa, sb = cos_ref[...], sa_ref[...], sb_ref[...]
    half = ROT_DIM // 2

    def norm_rope(z, g):
        zn = z * lax.rsqrt(jnp.mean(z * z, axis=-1, keepdims=True) + NORM_EPS) * g
        return zn * cos + pltpu.roll(zn, HEAD_DIM - half, 1) * sa + pltpu.roll(zn, half, 1) * sb

    def proj(c0, width):
        return jnp.dot(h, w_ref[:, c0:c0 + width], preferred_element_type=F32)

    c0 = Q_W + 2 * KV_W
    u = proj(c0, D_MODEL) * _sigmoid(proj(c0 + D_MODEL, D_MODEL))
    if fuse_conv:
        first = pl.program_id(0) % tiles_per_seq == 0

        @pl.when(first)
        def _():
            ext_ref[0:CONV_HALO_ROWS, :] = jnp.zeros((CONV_HALO_ROWS, D_MODEL), F32)

        @pl.when(jnp.logical_not(first))
        def _():
            ext_ref[0:CONV_HALO_ROWS, :] = ext_ref[tm:tm + CONV_HALO_ROWS, :]

        ext_ref[CONV_HALO_ROWS:, :] = u
        tail_out_ref[0] = u[tm - CONV_HALO_ROWS:]
        for sub in range(tm // CONV_TILE_ROWS):
            _causal_conv_tile(ext_ref, sh_ref, yc_ref, sub * CONV_TILE_ROWS, cw_ref, cb_ref)
            c_ref[sub * CONV_TILE_ROWS:(sub + 1) * CONV_TILE_ROWS, :] = _ln_swish(
                yc_ref[...], lg_ref[...], lb_ref[...]).astype(c_ref.dtype)
    else:
        u_ref[...] = u

    zq = proj(0, Q_W)
    qg = qg_ref[...]
    for hh in range(N_HEADS):
        sl = slice(hh * HEAD_DIM, (hh + 1) * HEAD_DIM)
        q_ref[:, sl] = (norm_rope(zq[:, sl], qg) * Q_PRESCALE).astype(BF16)

    zkv = proj(Q_W, 2 * KV_W)
    kg = kg_ref[...]
    n_blk = ksum_ref.shape[1]
    for g in range(N_KV_HEADS):
        sl = slice(g * HEAD_DIM, (g + 1) * HEAD_DIM)
        kk = norm_rope(zkv[:, sl], kg)
        vv = zkv[:, KV_W + g * HEAD_DIM:KV_W + (g + 1) * HEAD_DIM]
        k32_ref[pl.ds(g, tm, stride=N_KV_HEADS), :] = kk
        v32_ref[pl.ds(g, tm, stride=N_KV_HEADS), :] = vv
        kbf_ref[:, sl] = kk.astype(BF16)
        vbf_ref[:, sl] = vv.astype(BF16)
        for blk in range(n_blk):
            ksum_ref[0, blk:blk + 1, sl] = jnp.sum(
                kk[blk * MOBA_BLOCK:(blk + 1) * MOBA_BLOCK], axis=0, keepdims=True)

    sgc_ref[...] = _sigmoid(proj(c0 + 2 * D_MODEL, D_MODEL)).astype(BF16)
    sga_ref[...] = _sigmoid(proj(c0 + 3 * D_MODEL, D_MODEL)).astype(BF16)


def _in_proj(x2d, g1, w_bf, qg, kg, cos_t, sa_t, sb_t, conv=None, seq=None):
    m = x2d.shape[0]
    tm = IN_TILE_ROWS
    nt = m // tm
    ntab = cos_t.shape[0] // tm
    n_blk = tm // MOBA_BLOCK
    fuse_conv = conv is not None
    row = lambda i: (i, 0)
    tab = lambda i: (i % ntab, 0)
    rows = lambda w: pl.BlockSpec((tm, w), row)
    head_rows = pl.BlockSpec((tm * N_KV_HEADS, HEAD_DIM), row)
    out_shape = [
        jax.ShapeDtypeStruct((m, Q_W), BF16),
        jax.ShapeDtypeStruct((m * N_KV_HEADS, HEAD_DIM), F32),
        jax.ShapeDtypeStruct((m, KV_W), BF16),
        jax.ShapeDtypeStruct((m * N_KV_HEADS, HEAD_DIM), F32),
        jax.ShapeDtypeStruct((m, KV_W), BF16),
        jax.ShapeDtypeStruct((m, D_MODEL), BF16 if fuse_conv else F32),
        jax.ShapeDtypeStruct((m, D_MODEL), BF16),
        jax.ShapeDtypeStruct((m, D_MODEL), BF16),
        jax.ShapeDtypeStruct((nt, n_blk, KV_W), F32),
    ]
    in_specs = [rows(D_MODEL), _const_spec((1, D_MODEL)), _const_spec((D_MODEL, IN_W)),
                _const_spec((1, HEAD_DIM)), _const_spec((1, HEAD_DIM)),
                pl.BlockSpec((tm, HEAD_DIM), tab), pl.BlockSpec((tm, HEAD_DIM), tab),
                pl.BlockSpec((tm, HEAD_DIM), tab)]
    out_specs = [rows(Q_W), head_rows, rows(KV_W), head_rows, rows(KV_W), rows(D_MODEL),
                 rows(D_MODEL), rows(D_MODEL), pl.BlockSpec((1, n_blk, KV_W), lambda i: (i, 0, 0))]
    args = [x2d, g1, w_bf, qg, kg, cos_t, sa_t, sb_t]
    scratch_shapes = []
    tiles_per_seq = None
    if fuse_conv:
        tiles_per_seq = seq // tm
        in_specs += [_const_spec((CONV_WIDTH, D_MODEL))] + [_const_spec((1, D_MODEL))] * 3
        args += list(conv)
        out_shape.append(jax.ShapeDtypeStruct((m // seq, CONV_HALO_ROWS, D_MODEL), F32))
        out_specs.append(pl.BlockSpec((1, CONV_HALO_ROWS, D_MODEL), lambda i: (i // tiles_per_seq, 0, 0)))
        scratch_shapes = [
            pltpu.VMEM((tm + CONV_HALO_ROWS, D_MODEL), F32),
            pltpu.VMEM((SUBLANES - 1, CONV_TILE_ROWS + CONV_HALO_ROWS - SUBLANES, D_MODEL), F32),
            pltpu.VMEM((CONV_TILE_ROWS, D_MODEL), F32)]
    return pl.pallas_call(
        functools.partial(_in_proj_kernel, fuse_conv=fuse_conv, tiles_per_seq=tiles_per_seq),
        out_shape=tuple(out_shape),
        grid=(nt,),
        in_specs=in_specs,
        out_specs=tuple(out_specs),
        scratch_shapes=scratch_shapes,
        compiler_params=pltpu.CompilerParams(dimension_semantics=("arbitrary",),
                                             vmem_limit_bytes=VMEM_LIMIT_BYTES),
        name="in_proj",
    )(*args)


def _split3_bf16(a):
    a1 = a.astype(BF16)
    r1 = a - a1.astype(F32)
    a2 = r1.astype(BF16)
    a3 = (r1 - a2.astype(F32)).astype(BF16)
    return a1, a2, a3


def _block_scores(q_bf, means, blocks_first=False):
    if blocks_first:
        return sum(lax.dot_general(part, q_bf, NT_DIMS, preferred_element_type=F32)
                   for part in _split3_bf16(means))
    return sum(lax.dot_general(q_bf, part, NT_DIMS, preferred_element_type=F32)
               for part in _split3_bf16(means))


def _select_bias(scores, n_elig, block_axis):
    nb = scores.shape[block_axis]
    blk = lax.broadcasted_iota(jnp.int32, scores.shape, block_axis)
    sc = jnp.where(blk < n_elig, scores, NEG_INF)
    rank = jnp.zeros(scores.shape, F32)
    for c in range(nb):
        s_c = sc[c:c + 1, :] if block_axis == 0 else sc[:, c:c + 1]
        tie = jnp.where(blk > c, 1.0, 0.0)
        rank = rank + jnp.where(s_c > sc, 1.0, jnp.where(s_c == sc, tie, 0.0))
    keep = jnp.where(blk < n_elig, jnp.where(rank < MOBA_TOP_K, 1.0, 0.0), 0.0)
    return jnp.where(keep > 0.5, 0.0, NEG_INF)


def _attn_prompt_kernel(q_ref, k_ref, v_ref, ks_ref, o_ref, s_ref, acc_ref, mx_ref):
    i = pl.program_id(2)
    tq = q_ref.shape[0]
    nb = ks_ref.shape[1]
    q2 = jnp.concatenate([q_ref[:, :HEAD_DIM], q_ref[:, HEAD_DIM:]], axis=0)
    rows = q2.shape[0]
    means = ks_ref[0] * (1.0 / MOBA_BLOCK)
    bias_t = _select_bias(_block_scores(q2, means, blocks_first=True), i, 0)
    bias_pad = jnp.concatenate([bias_t, jnp.zeros((HEAD_DIM - nb, rows), F32)], axis=0)
    q_aug = jnp.concatenate([q2, bias_pad.T.astype(BF16)], axis=1)

    def block_start(j):
        return pl.multiple_of(j * MOBA_BLOCK, MOBA_BLOCK)

    def lane_max(mx, s):
        return jnp.maximum(mx, jnp.maximum(s[:, :HEAD_DIM], s[:, HEAD_DIM:]))

    def sweep(n, group_body, carry):
        done = 0
        for u in ATTN_BLOCK_GROUPS:
            count = (n - done) // u
            carry = lax.fori_loop(
                0, count, lambda t, c, done=done, u=u: group_body(done + t * u, u, c), carry)
            done = done + count * u
        return carry

    def past_logits(j0, u, mx):
        kk = k_ref[pl.ds(block_start(j0), u * MOBA_BLOCK), :]
        shape = (u * MOBA_BLOCK, HEAD_DIM)
        blk = j0 + (lax.broadcasted_iota(jnp.int32, shape, 0) >> MOBA_SHIFT)
        lane = lax.broadcasted_iota(jnp.int32, shape, 1)
        pick = jnp.where(lane == blk, 1.0, 0.0).astype(BF16)
        s = lax.dot_general(q_aug, jnp.concatenate([kk, pick], axis=1), NT_DIMS,
                            preferred_element_type=F32)
        group_max = mx_ref[...]
        for w in range(u):
            sw = s[:, w * MOBA_BLOCK:(w + 1) * MOBA_BLOCK]
            s_ref[j0 + w] = sw
            group_max = lane_max(group_max, sw)
        mx_ref[...] = group_max
        return mx

    ki = k_ref[pl.ds(block_start(i), MOBA_BLOCK), :]
    s = lax.dot_general(q2, ki, NT_DIMS, preferred_element_type=F32)
    t = lax.broadcasted_iota(jnp.int32, s.shape, 0) % tq
    kpos = lax.broadcasted_iota(jnp.int32, s.shape, 1)
    s = jnp.where(kpos <= t, s, NEG_INF)
    s_ref[i] = s
    mx_ref[...] = lane_max(jnp.full((rows, HEAD_DIM), NEG_INF, F32), s)

    sweep(i, past_logits, 0)
    m = jnp.broadcast_to(jnp.max(mx_ref[...], axis=-1, keepdims=True), (rows, HEAD_DIM))

    def weighted_values(j0, u, acc):
        parts = []
        for w in range(u):
            sw = s_ref[j0 + w]
            parts += [jnp.exp2(sw[:, :HEAD_DIM] - m), jnp.exp2(sw[:, HEAD_DIM:] - m)]
        p = jnp.concatenate(parts, axis=1).astype(BF16)
        vv = v_ref[pl.ds(block_start(j0), u * MOBA_BLOCK), :]
        ones = jnp.ones((u * MOBA_BLOCK, HEAD_DIM), BF16)
        acc_ref[...] += jnp.dot(p, jnp.concatenate([vv, ones], axis=1), preferred_element_type=F32)
        return acc

    acc_ref[...] = jnp.zeros(acc_ref.shape, F32)
    sweep(i + 1, weighted_values, 0)
    acc = acc_ref[...]
    out = acc[:, :HEAD_DIM] / acc[:, HEAD_DIM:]
    o_ref[:, :HEAD_DIM] = out[:tq].astype(o_ref.dtype)
    o_ref[:, HEAD_DIM:] = out[tq:].astype(o_ref.dtype)


def _attn_prompt(q_bf, k_bf, v_bf, ksum, batch, seq):
    nb = seq // MOBA_BLOCK
    gw = KV_GROUP * HEAD_DIM
    return pl.pallas_call(
        _attn_prompt_kernel,
        out_shape=jax.ShapeDtypeStruct(q_bf.shape, BF16),
        grid=(batch, N_KV_HEADS, nb),
        in_specs=[pl.BlockSpec((MOBA_BLOCK, gw), lambda b, g, i: (b * nb + i, g)),
                  pl.BlockSpec((seq, HEAD_DIM), lambda b, g, i: (b, g)),
                  pl.BlockSpec((seq, HEAD_DIM), lambda b, g, i: (b, g)),
                  pl.BlockSpec((1, nb, HEAD_DIM), lambda b, g, i: (b, 0, g))],
        out_specs=pl.BlockSpec((MOBA_BLOCK, gw), lambda b, g, i: (b * nb + i, g)),
        scratch_shapes=[pltpu.VMEM((nb, KV_GROUP * MOBA_BLOCK, 2 * HEAD_DIM), F32),
                        pltpu.VMEM((KV_GROUP * MOBA_BLOCK, 2 * HEAD_DIM), F32),
                        pltpu.VMEM((KV_GROUP * MOBA_BLOCK, HEAD_DIM), F32)],
        compiler_params=pltpu.CompilerParams(
            dimension_semantics=("arbitrary", "arbitrary", "arbitrary"),
            vmem_limit_bytes=VMEM_LIMIT_BYTES),
        name="attn_prompt",
    )(q_bf, k_bf, v_bf, ksum)


def _attn_sample_kernel(pt_ref, q_ref, kn_ref, vn_ref, ck_hbm, cv_hbm, o_ref,
                        ring, sems, vbuf, vsems, logit_ref, sum_ref, p_ref, pc_ref, own_ref, den_ref,
                        nsel_ref, *, n_seq, n_pages):
    s = pl.program_id(0)
    cp = SAMPLE_CHUNK_PAGES
    n_chunks = n_pages // cp
    chunk_keys = cp * PAGE_SIZE
    blk_per_chunk = chunk_keys // MOBA_BLOCK
    pages_per_blk = MOBA_BLOCK // PAGE_SIZE
    n_blocks = n_pages * PAGE_SIZE // MOBA_BLOCK
    n_q = q_ref.shape[2]
    t_new = kn_ref.shape[2]
    page_rows = PAGE_SIZE * N_KV_HEADS
    max_sel = pc_ref.shape[1]
    assert n_chunks % SAMPLE_RING == 0 and n_blocks <= 32 and max_sel % SAMPLE_V_GROUP == 0

    def page_copy(seq, c, r):
        page = pt_ref[seq, c * cp + r]
        slot = c % SAMPLE_RING
        return pltpu.make_async_copy(ck_hbm.at[page], ring.at[slot, pl.ds(r * page_rows, page_rows)],
                                     sems.at[slot])

    def v_copy(g, blk, slot, r):
        page = pt_ref[s, blk * pages_per_blk + r]
        return pltpu.make_async_copy(cv_hbm.at[page, :, g, :],
                                     vbuf.at[g * max_sel + slot, pl.ds(r * PAGE_SIZE, PAGE_SIZE)],
                                     vsems.at[g])

    def head_rows(slot, g):
        return ring.at[slot][pl.ds(g, chunk_keys, stride=N_KV_HEADS), :]

    def start_chunk(seq, c):
        for r in range(cp):
            page_copy(seq, c, r).start()

    def wait_chunk(seq, c):
        for r in range(cp):
            page_copy(seq, c, r).wait()

    depth = SAMPLE_RING - 1

    @pl.when(s == 0)
    def _():
        for c in range(depth):
            start_chunk(s, c)

        def clear(t, carry):
            vbuf[t] = jnp.zeros(vbuf.shape[1:], F32)
            return carry

        lax.fori_loop(0, vbuf.shape[0], clear, 0)

    def pick_and_fetch():
        qs = [q_ref[0, g] for g in range(N_KV_HEADS)]
        scores = jnp.concatenate(
            [_block_scores(qs[g], sum_ref[:, g * HEAD_DIM:(g + 1) * HEAD_DIM] * (1.0 / MOBA_BLOCK))
             for g in range(N_KV_HEADS)], axis=0)
        bias_all = _select_bias(scores, n_blocks, 1)
        picked = jnp.where(bias_all == 0.0, 1.0, 0.0)
        e_shape = (n_blocks, n_blocks * MOBA_BLOCK)
        expand = jnp.where(
            (lax.broadcasted_iota(jnp.int32, e_shape, 1) >> MOBA_SHIFT)
            == lax.broadcasted_iota(jnp.int32, e_shape, 0), 1.0, 0.0).astype(BF16)
        blk_lane = lax.broadcasted_iota(jnp.int32, (1, n_blocks), 1)
        half_weight = jnp.left_shift(1, blk_lane & 15).astype(F32)
        low_weight = jnp.where(blk_lane < 16, half_weight, 0.0)
        high_weight = jnp.where(blk_lane >= 16, half_weight, 0.0)

        def to_scalar(v):
            return jnp.sum(v, axis=-1, keepdims=True).astype(jnp.int32)[0, 0]

        masks = []
        for g in range(N_KV_HEADS):
            rows_g = slice(g * n_q, (g + 1) * n_q)
            any_row = jnp.max(picked[rows_g], axis=0, keepdims=True)
            masks.append(to_scalar(any_row * low_weight) | (to_scalar(any_row * high_weight) << 16))
            bias = jnp.dot(bias_all[rows_g].astype(BF16), expand, preferred_element_type=F32)
            qf = qs[g].astype(F32)
            kn = kn_ref[0, g]
            trow = lax.broadcasted_iota(jnp.int32, (n_q, 1), 0) % t_new
            own = []
            for tk in range(t_new):
                lo = jnp.sum(qf * kn[tk:tk + 1, :], axis=-1, keepdims=True)
                own.append(jnp.where(trow >= tk, lo, NEG_INF))
            mx = own[0]
            for lo in own[1:]:
                mx = jnp.maximum(mx, lo)
            lm = logit_ref[g] + bias
            mx = jnp.maximum(mx, jnp.max(lm, axis=-1, keepdims=True))
            p = jnp.exp2(lm - mx)
            den = jnp.sum(p, axis=-1, keepdims=True)
            vn = vn_ref[0, g]
            own_out = jnp.zeros((n_q, HEAD_DIM), F32)
            for tk in range(t_new):
                p_own = jnp.exp2(own[tk] - mx)
                den = den + p_own
                own_out = own_out + p_own * vn[tk:tk + 1, :]
            own_ref[g] = own_out
            den_ref[g] = jnp.broadcast_to(den, (n_q, HEAD_DIM))
            for b in range(n_blocks):
                p_ref[g, b] = p[:, b * MOBA_BLOCK:(b + 1) * MOBA_BLOCK]
            pc_ref[g] = jnp.zeros(pc_ref.shape[1:], F32)

        for g in range(N_KV_HEADS):
            def issue(blk, count, g=g):
                bit = (masks[g] >> blk) & 1

                @pl.when(bit == 1)
                def _():
                    for r in range(pages_per_blk):
                        v_copy(g, blk, count, r).start()
                    pc_ref[g, count] = p_ref[g, blk]

                return count + bit

            nsel_ref[g] = lax.fori_loop(0, n_blocks, issue, jnp.int32(0))

    def stream_keys():
        qs = [q_ref[0, g] for g in range(N_KV_HEADS)]
        for c in range(n_chunks):
            wait_chunk(s, c)
            nxt = c + depth
            if nxt < n_chunks:
                start_chunk(s, nxt)
            else:
                @pl.when(s + 1 < n_seq)
                def _():
                    start_chunk(s + 1, nxt - n_chunks)
            slot = c % SAMPLE_RING
            for g in range(N_KV_HEADS):
                kc = head_rows(slot, g)
                for b in range(blk_per_chunk):
                    row = c * blk_per_chunk + b
                    sum_ref[row:row + 1, g * HEAD_DIM:(g + 1) * HEAD_DIM] = jnp.sum(
                        kc[b * MOBA_BLOCK:(b + 1) * MOBA_BLOCK], axis=0, keepdims=True)
                logit_ref[g, :, c * chunk_keys:(c + 1) * chunk_keys] = lax.dot_general(
                    qs[g], kc.astype(BF16), NT_DIMS, preferred_element_type=F32)

    def weighted_values():
        for g in range(N_KV_HEADS):
            def wait_block(t, carry, g=g):
                for r in range(pages_per_blk):
                    v_copy(g, 0, 0, r).wait()
                return carry

            lax.fori_loop(0, nsel_ref[g], wait_block, 0)
            acc = own_ref[g]
            for t in range(max_sel // SAMPLE_V_GROUP):
                first = t * SAMPLE_V_GROUP
                pg = jnp.concatenate([pc_ref[g, first + w] for w in range(SAMPLE_V_GROUP)], axis=1)
                vg = vbuf[g * max_sel + first:g * max_sel + first + SAMPLE_V_GROUP]
                vg = vg.reshape(SAMPLE_V_GROUP * MOBA_BLOCK, HEAD_DIM)
                acc = acc + jnp.dot(pg.astype(BF16), vg.astype(BF16), preferred_element_type=F32)
            o_ref[0, g] = acc / den_ref[g]

    stream_keys()
    pick_and_fetch()
    weighted_values()


def _attn_sample(page_table, q_s, k_new, v_new, cache_k_rows, cache_v):
    n_seq, n_pages = page_table.shape
    n_q, t_new = q_s.shape[2], k_new.shape[2]
    chunk_keys = SAMPLE_CHUNK_PAGES * PAGE_SIZE
    n_blocks = n_pages * PAGE_SIZE // MOBA_BLOCK
    max_sel = min(n_q * MOBA_TOP_K, n_blocks)
    max_sel = -(-max_sel // SAMPLE_V_GROUP) * SAMPLE_V_GROUP
    per_seq = lambda shape: pl.BlockSpec((1,) + shape, lambda s, pt: (s, 0, 0, 0))
    grid_spec = pltpu.PrefetchScalarGridSpec(
        num_scalar_prefetch=1,
        grid=(n_seq,),
        in_specs=[per_seq((N_KV_HEADS, n_q, HEAD_DIM)), per_seq((N_KV_HEADS, t_new, HEAD_DIM)),
                  per_seq((N_KV_HEADS, t_new, HEAD_DIM)),
                  pl.BlockSpec(memory_space=pl.ANY), pl.BlockSpec(memory_space=pl.ANY)],
        out_specs=per_seq((N_KV_HEADS, n_q, HEAD_DIM)),
        scratch_shapes=[pltpu.VMEM((SAMPLE_RING, chunk_keys * N_KV_HEADS, HEAD_DIM), F32),
                        pltpu.SemaphoreType.DMA((SAMPLE_RING,)),
                        pltpu.VMEM((N_KV_HEADS * max_sel, MOBA_BLOCK, HEAD_DIM), F32),
                        pltpu.SemaphoreType.DMA((N_KV_HEADS,)),
                        pltpu.VMEM((N_KV_HEADS, n_q, n_pages * PAGE_SIZE), F32),
                        pltpu.VMEM((n_blocks, KV_W), F32),
                        pltpu.VMEM((N_KV_HEADS, n_blocks, n_q, MOBA_BLOCK), F32),
                        pltpu.VMEM((N_KV_HEADS, max_sel, n_q, MOBA_BLOCK), F32),
                        pltpu.VMEM((N_KV_HEADS, n_q, HEAD_DIM), F32),
                        pltpu.VMEM((N_KV_HEADS, n_q, HEAD_DIM), F32),
                        pltpu.SMEM((N_KV_HEADS,), jnp.int32)],
    )
    return pl.pallas_call(
        functools.partial(_attn_sample_kernel, n_seq=n_seq, n_pages=n_pages),
        out_shape=jax.ShapeDtypeStruct(q_s.shape, F32),
        grid_spec=grid_spec,
        compiler_params=pltpu.CompilerParams(dimension_semantics=("arbitrary",),
                                             vmem_limit_bytes=VMEM_LIMIT_BYTES),
        name="attn_sample",
    )(page_table, q_s, k_new, v_new, cache_k_rows, cache_v)


def _conv_sample_kernel(st_ref, u0_ref, u1_ref, u2_ref, u3_ref, wst_ref, wnew_ref, cb_ref, lg_ref,
                        lb_ref, o0_ref, o1_ref, o2_ref, o3_ref):
    st = st_ref[...]
    new = [u0_ref[...], u1_ref[...], u2_ref[...], u3_ref[...]]
    outs = [o0_ref, o1_ref, o2_ref, o3_ref]
    for t in range(len(new)):
        yc = jnp.sum(st * wst_ref[t][None, :, :], axis=1) + cb_ref[...]
        for tk in range(t + 1):
            yc = yc + new[tk] * wnew_ref[t, tk:tk + 1, :]
        outs[t][...] = _ln_swish(yc, lg_ref[...], lb_ref[...]).astype(outs[t].dtype)


def _conv_sample(state, u_new, conv_w, conv_b, ln_g, ln_b):
    n_seq = state.shape[0]
    t_new = len(u_new)
    ts = 16
    w_state = jnp.stack([jnp.concatenate([jnp.zeros((t, D_MODEL), F32), conv_w[:CONV_STATE - t]], axis=0)
                         for t in range(t_new)])
    w_new = jnp.stack([jnp.concatenate([conv_w[CONV_STATE - t:], jnp.zeros((t_new - 1 - t, D_MODEL), F32)],
                                       axis=0) for t in range(t_new)])
    seqs = pl.BlockSpec((ts, D_MODEL), lambda i: (i, 0))
    return pl.pallas_call(
        _conv_sample_kernel,
        out_shape=tuple(jax.ShapeDtypeStruct((n_seq, D_MODEL), BF16) for _ in range(t_new)),
        grid=(n_seq // ts,),
        in_specs=[pl.BlockSpec((ts, CONV_STATE, D_MODEL), lambda i: (i, 0, 0))] + [seqs] * t_new + [
            _const_spec((t_new, CONV_STATE, D_MODEL)), _const_spec((t_new, t_new, D_MODEL)),
            _const_spec((1, D_MODEL)), _const_spec((1, D_MODEL)), _const_spec((1, D_MODEL))],
        out_specs=tuple(seqs for _ in range(t_new)),
        compiler_params=pltpu.CompilerParams(dimension_semantics=("arbitrary",),
                                             vmem_limit_bytes=VMEM_LIMIT_BYTES),
        name="conv_sample",
    )(state, *u_new, w_state, w_new, conv_b, ln_g, ln_b)


def _post_kernel(c_ref, sgc_ref, sga_ref, o_ref, x_ref, wco_ref, wo_ref, g2_ref, wfi_ref, wfo_ref,
                 y_ref):
    yc = jnp.dot(c_ref[...], wco_ref[...], preferred_element_type=F32)
    m = sgc_ref[...].astype(F32) * yc + sga_ref[...].astype(F32) * o_ref[...].astype(F32)
    x1 = x_ref[...] + jnp.dot(m.astype(BF16), wo_ref[...], preferred_element_type=F32)
    h2 = (x1 * lax.rsqrt(jnp.mean(x1 * x1, axis=-1, keepdims=True) + NORM_EPS) * g2_ref[...]).astype(BF16)
    ab = jnp.dot(h2, wfi_ref[...], preferred_element_type=F32)
    a, b = ab[:, :FFN_HIDDEN], ab[:, FFN_HIDDEN:]
    hid = (a * _sigmoid(a) * b).astype(BF16)
    y_ref[...] = x1 + jnp.dot(hid, wfo_ref[...], preferred_element_type=F32)


def _post(c_act, sgc, sga, o_attn, x2d, wco_bf, wo_bf, g2, wfi_bf, wfo_bf):
    m = x2d.shape[0]
    tm = POST_TILE_ROWS
    rows = pl.BlockSpec((tm, D_MODEL), lambda i: (i, 0))
    return pl.pallas_call(
        _post_kernel,
        out_shape=jax.ShapeDtypeStruct((m, D_MODEL), F32),
        grid=(m // tm,),
        in_specs=[rows, rows, rows, rows, rows,
                  _const_spec((D_MODEL, D_MODEL)), _const_spec((D_MODEL, D_MODEL)),
                  _const_spec((1, D_MODEL)), _const_spec((D_MODEL, 2 * FFN_HIDDEN)),
                  _const_spec((FFN_HIDDEN, D_MODEL))],
        out_specs=rows,
        compiler_params=pltpu.CompilerParams(dimension_semantics=("arbitrary",),
                                             vmem_limit_bytes=VMEM_LIMIT_BYTES),
        name="post",
    )(c_act, sgc, sga, o_attn, x2d, wco_bf, wo_bf, g2, wfi_bf, wfo_bf)


def _rope_tables(pos):
    half = ROT_DIM // 2
    inv_freq = ROPE_THETA ** (-(jnp.arange(half, dtype=F32) * 2.0 / ROT_DIM))
    ang = pos.astype(F32)[:, None] * inv_freq[None, :]
    cos, sin = jnp.cos(ang), jnp.sin(ang)
    n = pos.shape[0]
    zeros = lambda w: jnp.zeros((n, w), F32)
    cos_t = jnp.concatenate([cos, cos, jnp.ones((n, HEAD_DIM - ROT_DIM), F32)], axis=1)
    sa_t = jnp.concatenate([-sin, zeros(HEAD_DIM - half)], axis=1)
    sb_t = jnp.concatenate([zeros(half), sin, zeros(HEAD_DIM - ROT_DIM)], axis=1)
    return cos_t, sa_t, sb_t


def kernel(x_prompt, x_sample, cache_k, cache_v, state_conv, page_table, norm1_g, w_in, q_norm_g,
           k_norm_g, conv_dw_w, conv_dw_b, conv_ln_g, conv_ln_b, w_conv_out, w_out, norm2_g,
           w_ffn_in, w_ffn_out):
    batch, seq, _ = x_prompt.shape
    n_seq, t_new, _ = x_sample.shape
    n_phys = cache_k.shape[0]
    past_len = page_table.shape[1] * PAGE_SIZE
    assert seq % IN_TILE_ROWS == 0 and (n_seq * t_new) % IN_TILE_ROWS == 0
    assert IN_TILE_ROWS % t_new == 0 and past_len % MOBA_BLOCK == 0 and seq >= CONV_STATE

    row = lambda v: v.reshape(1, -1).astype(F32)
    w_in_bf, wco_bf, wo_bf = w_in.astype(BF16), w_conv_out.astype(BF16), w_out.astype(BF16)
    wfi_bf, wfo_bf = w_ffn_in.astype(BF16), w_ffn_out.astype(BF16)
    g1, g2, qg, kg = row(norm1_g), row(norm2_g), row(q_norm_g), row(k_norm_g)
    conv_b, ln_g, ln_b = row(conv_dw_b), row(conv_ln_g), row(conv_ln_b)

    xp = x_prompt.reshape(batch * seq, D_MODEL)
    q_p, k_p, kbf_p, v_p, vbf_p, c_p, sgc_p, sga_p, ksum_p, u_tail_p = _in_proj(
        xp, g1, w_in_bf, qg, kg, *_rope_tables(jnp.arange(seq)),
        conv=(conv_dw_w, conv_b, ln_g, ln_b), seq=seq)
    nb = seq // MOBA_BLOCK
    o_p = _attn_prompt(q_p, kbf_p, vbf_p, ksum_p.reshape(batch, nb, KV_W), batch, seq)
    y_p = _post(c_p, sgc_p, sga_p, o_p, xp, wco_bf, wo_bf, g2, wfi_bf, wfo_bf)

    xs = x_sample.reshape(n_seq * t_new, D_MODEL)
    pos_s = past_len + jnp.arange(IN_TILE_ROWS) % t_new
    q_s, k_s, _, v_s, _, u_s, sgc_s, sga_s, _ = _in_proj(
        xs, g1, w_in_bf, qg, kg, *_rope_tables(pos_s))
    q_g = q_s.reshape(n_seq, t_new, N_KV_HEADS, KV_GROUP, HEAD_DIM).transpose(0, 2, 3, 1, 4)
    q_g = q_g.reshape(n_seq, N_KV_HEADS, KV_GROUP * t_new, HEAD_DIM)
    k_new = k_s.reshape(n_seq, t_new, N_KV_HEADS, HEAD_DIM).transpose(0, 2, 1, 3)
    v_new = v_s.reshape(n_seq, t_new, N_KV_HEADS, HEAD_DIM).transpose(0, 2, 1, 3)
    page_rows = PAGE_SIZE * N_KV_HEADS
    o_g = _attn_sample(page_table, q_g, k_new, v_new, cache_k.reshape(n_phys, page_rows, HEAD_DIM),
                       cache_v)
    o_s = o_g.reshape(n_seq, N_KV_HEADS, KV_GROUP, t_new, HEAD_DIM).transpose(0, 3, 1, 2, 4)
    o_s = o_s.reshape(n_seq * t_new, D_MODEL).astype(BF16)
    u_s3 = u_s.reshape(n_seq, t_new, D_MODEL)
    c_steps = _conv_sample(state_conv, [u_s3[:, t] for t in range(t_new)], conv_dw_w, conv_b, ln_g, ln_b)
    c_s = jnp.stack(c_steps, axis=1).reshape(n_seq * t_new, D_MODEL)
    y_s = _post(c_s, sgc_s, sga_s, o_s, xs, wco_bf, wo_bf, g2, wfi_bf, wfo_bf)

    kv_shape_p = (batch, seq, N_KV_HEADS, HEAD_DIM)
    kv_shape_s = (n_seq, t_new, N_KV_HEADS, HEAD_DIM)
    conv_prompt_state = u_tail_p[:, CONV_HALO_ROWS - CONV_STATE:]
    conv_sample_state = jnp.concatenate([state_conv, u_s3], axis=1)[:, t_new:]
    return (y_p.reshape(batch, seq, D_MODEL), y_s.reshape(n_seq, t_new, D_MODEL),
            k_p.reshape(kv_shape_p), v_p.reshape(kv_shape_p), conv_prompt_state,
            k_s.reshape(kv_shape_s), v_s.reshape(kv_shape_s), conv_sample_state)
```
